```python
import math
import jax
import jax.numpy as jnp
from jax import lax
import numpy as np

D_MODEL = 1024
BATCH = 4
SEQ = 4096
DEPTH = 2
DEC_BATCH = 32
DEC_SEQ = 1
PAST_LEN = 8192
PAGE_SIZE = 128

HEAD_DIM = 64
A_HEADS = 4
A_KV_HEADS = 2
IDX_HEADS = 4
IDX_DIM = 64
TOPK_MAX = 256
B_HEADS = 4
C_HEADS = 4
D_HEADS = 4
D_GROUPS = 2
D_STATE = 128
D_CONV = 4
D_FF = 2816
FFN_CONV = 3
N_BRANCH = 4
ROPE_THETA = 10000.0
Q_BLOCK = 128
CHUNK = 128
EPS = 1e-6
NEG = -1e30

A_Q = A_HEADS * HEAD_DIM
A_KV = A_KV_HEADS * HEAD_DIM
A_IQ = IDX_HEADS * IDX_DIM
B_W = B_HEADS * HEAD_DIM
C_W = C_HEADS * HEAD_DIM
D_W = D_HEADS * HEAD_DIM
D_BC = D_GROUPS * D_STATE
D_CONV_CH = D_W + 2 * D_BC
IN_SPLITS = (A_Q, A_KV, A_KV, A_IQ, IDX_DIM, IDX_HEADS, B_W, B_W, B_W, C_W, C_W, C_W, C_HEADS, C_HEADS, C_W, D_W, D_CONV_CH, D_HEADS, N_BRANCH * D_MODEL)
IN_W = sum(IN_SPLITS)

kernel_name = 'hybrid_dsa_stickbreak_mlstm_ssd_convffn_step'

f32 = jnp.float32


def rms_norm(x, w):
    xf = x.astype(f32)
    y = xf * lax.rsqrt(jnp.mean(xf * xf, axis=-1, keepdims=True) + EPS)
    return (y * w.astype(f32)).astype(x.dtype)


def rope(x, pos):
    half = x.shape[-1] // 2
    inv = jnp.power(ROPE_THETA, -jnp.arange(half, dtype=f32) / half)
    ang = pos.astype(f32)[:, None] * inv[None, :]
    cos = jnp.cos(ang)[None, :, None, :]
    sin = jnp.sin(ang)[None, :, None, :]
    xf = x.astype(f32)
    x1, x2 = xf[..., :half], xf[..., half:]
    return jnp.concatenate([x1 * cos - x2 * sin, x2 * cos + x1 * sin], axis=-1).astype(x.dtype)


def causal_dwconv(x, buf, w, b):
    width, t = w.shape[0], x.shape[1]
    xx = jnp.concatenate([buf.astype(x.dtype), x], axis=1)
    y = b + xx[:, 0:t] * w[0]
    for j in range(1, width):
        y = y + xx[:, j:j + t] * w[j]
    return y, xx[:, t:]


def to_chunks(a, c):
    return jnp.moveaxis(a.reshape((a.shape[0], a.shape[1] // c, c) + a.shape[2:]), 1, 0)


def from_chunks(a):
    a = jnp.moveaxis(a, 0, 1)
    return a.reshape((a.shape[0], a.shape[1] * a.shape[2]) + a.shape[3:])


def sweep_query_blocks(fn, batched, pos):
    t = pos.shape[0]
    blk = min(Q_BLOCK, t)
    xs = tuple(to_chunks(a, blk) for a in batched) + (pos.reshape(t // blk, blk),)
    return from_chunks(lax.map(lambda args: fn(*args), xs))


def take_rows(rows, idx):
    return jax.vmap(lambda r, i: r[i])(rows, idx)


def gather_past(pool, page_table):
    g = pool[page_table]
    return g.reshape((g.shape[0], g.shape[1] * g.shape[2]) + g.shape[3:])


def gather_rows(pool, page_table, new_rows, idx):
    ps = pool.shape[1]
    past = page_table.shape[1] * ps
    is_past = idx < past
    lp = jnp.minimum(idx, past - 1)
    phys = jax.vmap(lambda pt, l: pt[l // ps])(page_table, lp)
    from_pool = pool[phys, lp % ps]
    from_new = take_rows(new_rows, jnp.clip(idx - past, 0, new_rows.shape[1] - 1))
    sel = is_past.reshape(is_past.shape + (1,) * (pool.ndim - 2))
    return jnp.where(sel, from_pool.astype(from_new.dtype), from_new)


def dsa_block(q, iq, iw, qpos, kidx, gather_kv, topk):
    bsz, tb = q.shape[0], q.shape[1]
    s = jnp.einsum('bthd,bld->bthl', iq.astype(f32), kidx.astype(f32))
    score = jnp.einsum('bth,bthl->btl', iw.astype(f32), jax.nn.relu(s))
    admissible = jnp.arange(kidx.shape[1])[None, :] <= qpos[:, None]
    score = jnp.where(admissible[None], score, NEG)
    _, idx = lax.top_k(score, topk)
    valid = idx <= qpos[None, :, None]
    k_sel, v_sel = gather_kv(idx)
    qg = q.reshape(bsz, tb, A_KV_HEADS, A_HEADS // A_KV_HEADS, HEAD_DIM).astype(f32)
    logits = jnp.einsum('btjgd,btnjd->btjgn', qg, k_sel.astype(f32)) * (HEAD_DIM ** -0.5)
    logits = jnp.where(valid[:, :, None, None, :], logits, NEG)
    p = jax.nn.softmax(logits, axis=-1)
    o = jnp.einsum('btjgn,btnjd->btjgd', p, v_sel.astype(f32))
    return o.reshape(bsz, tb, A_Q).astype(q.dtype)


def stickbreak_block(q, qpos, k, v):
    bsz, tb = q.shape[0], q.shape[1]
    z = jnp.einsum('bthd,blhd->bhtl', q.astype(f32), k.astype(f32)) * (HEAD_DIM ** -0.5)
    mask = (jnp.arange(k.shape[1])[None, :] < qpos[:, None])[None, None]
    log_keep = jnp.where(mask, jax.nn.log_sigmoid(-z), 0.0)
    after = lax.cumsum(log_keep, axis=3, reverse=True) - log_keep
    a = jnp.where(mask, jnp.exp(jax.nn.log_sigmoid(z) + after), 0.0)
    o = jnp.einsum('bhtl,blhd->bthd', a, v.astype(f32))
    return o.reshape(bsz, tb, B_W).astype(q.dtype)


def mlstm_scan(q, k, v, ig, lf, c0, n0, m0):
    c = min(CHUNK, q.shape[1])
    causal = jnp.tril(jnp.ones((c, c), dtype=bool))[None, :, :, None]

    def step(carry, xs):
        cm, nv, m = carry
        qc, kc, vc, ic, fc = xs
        b = jnp.cumsum(fc, axis=1)
        logw = jnp.where(causal, b[:, :, None, :] - b[:, None, :, :] + ic[:, None, :, :], -jnp.inf)
        log_inter = b + m[:, None, :]
        mt = jnp.maximum(log_inter, jnp.max(logw, axis=2))
        sc = jnp.einsum('bthd,bshd->btsh', qc, kc) * jnp.exp(logw - mt[:, :, None, :])
        w_inter = jnp.exp(log_inter - mt)
        num = jnp.einsum('btsh,bshe->bthe', sc, vc) + w_inter[..., None] * jnp.einsum('bhed,bthd->bthe', cm, qc)
        den = jnp.sum(sc, axis=2) + w_inter * jnp.einsum('bhd,bthd->bth', nv, qc)
        h = num / jnp.maximum(jnp.abs(den), jnp.exp(-mt))[..., None]
        b_end = b[:, -1, :]
        log_u = b_end[:, None, :] - b + ic
        m_new = jnp.maximum(b_end + m, jnp.max(log_u, axis=1))
        u = jnp.exp(log_u - m_new[:, None, :])
        decay = jnp.exp(b_end + m - m_new)
        cm_new = decay[..., None, None] * cm + jnp.einsum('bsh,bshe,bshd->bhed', u, vc, kc)
        nv_new = decay[..., None] * nv + jnp.einsum('bsh,bshd->bhd', u, kc)
        return (cm_new, nv_new, m_new), h

    (c1, n1, m1), h = lax.scan(step, (c0, n0, m0), tuple(to_chunks(a, c) for a in (q, k, v, ig, lf)))
    return from_chunks(h), c1, n1, m1


def ssd_scan(x, dt, a_neg, bm, cm, h0):
    c = min(CHUNK, x.shape[1])
    causal = jnp.tril(jnp.ones((c, c), dtype=bool))[None, :, :, None]

    def step(h, xs):
        xc, dtc, bc, cc = xs
        acum = jnp.cumsum(dtc * a_neg, axis=1)
        decay = jnp.exp(jnp.where(causal, acum[:, :, None, :] - acum[:, None, :, :], -jnp.inf))
        sc = jnp.einsum('bthn,bshn->btsh', cc, bc) * decay
        y = jnp.einsum('btsh,bsh,bshp->bthp', sc, dtc, xc) + jnp.exp(acum)[..., None] * jnp.einsum('bthn,bhpn->bthp', cc, h)
        a_end = acum[:, -1, :]
        wts = jnp.exp(a_end[:, None, :] - acum) * dtc
        h_new = jnp.exp(a_end)[..., None, None] * h + jnp.einsum('bsh,bshn,bshp->bhpn', wts, bc, xc)
        return h_new, y

    h1, y = lax.scan(step, h0, tuple(to_chunks(a, c) for a in (x, dt, bm, cm)))
    return from_chunks(y), h1


def token_mix(xn, pos, w, st):
    bsz, t = xn.shape[0], xn.shape[1]
    offs = np.cumsum(IN_SPLITS)[:-1].tolist()
    (a_q, a_k, a_v, a_iq, a_ik, a_iw, b_q, b_k, b_v, c_q, c_k, c_v, c_i, c_f, c_o,
     d_z, d_xbc, d_dt, gate) = jnp.split(xn @ w['w_in'], offs, axis=-1)

    qa = rope(a_q.reshape(bsz, t, A_HEADS, HEAD_DIM), pos)
    ka = rope(a_k.reshape(bsz, t, A_KV_HEADS, HEAD_DIM), pos)
    va = a_v.reshape(bsz, t, A_KV_HEADS, HEAD_DIM)
    iq = rope(a_iq.reshape(bsz, t, IDX_HEADS, IDX_DIM), pos)
    ik = rope(a_ik[:, :, None, :], pos)[:, :, 0, :]
    iw = a_iw * (IDX_HEADS ** -0.5 * IDX_DIM ** -0.5)
    if st is None:
        kidx_all = ik

        def gather_kv(idx):
            return take_rows(ka, idx), take_rows(va, idx)
    else:
        pt = st['page_table']
        kidx_all = jnp.concatenate([gather_past(st['a_kidx'], pt).astype(ik.dtype), ik], axis=1)

        def gather_kv(idx):
            return gather_rows(st['a_k'], pt, ka, idx), gather_rows(st['a_v'], pt, va, idx)
    topk = min(TOPK_MAX, kidx_all.shape[1] // 4)
    o_a = sweep_query_blocks(lambda q_, iq_, iw_, qp: dsa_block(q_, iq_, iw_, qp, kidx_all, gather_kv, topk), (qa, iq, iw), pos)

    qb = b_q.reshape(bsz, t, B_HEADS, HEAD_DIM)
    kb = b_k.reshape(bsz, t, B_HEADS, HEAD_DIM)
    vb = b_v.reshape(bsz, t, B_HEADS, HEAD_DIM)
    if st is None:
        kb_all, vb_all = kb, vb
    else:
        kb_all = jnp.concatenate([gather_past(st['b_k'], pt).astype(kb.dtype), kb], axis=1)
        vb_all = jnp.concatenate([gather_past(st['b_v'], pt).astype(vb.dtype), vb], axis=1)
    o_b = sweep_query_blocks(lambda q_, qp: stickbreak_block(q_, qp, kb_all, vb_all), (qb,), pos)

    qc = c_q.reshape(bsz, t, C_HEADS, HEAD_DIM).astype(f32)
    kc = c_k.reshape(bsz, t, C_HEADS, HEAD_DIM).astype(f32) * (HEAD_DIM ** -0.5)
    vc = c_v.reshape(bsz, t, C_HEADS, HEAD_DIM).astype(f32)
    ig = (c_i + w['c_b_i']).astype(f32)
    lf = jax.nn.log_sigmoid((c_f + w['c_b_f']).astype(f32))
    if st is None:
        c0 = jnp.zeros((bsz, C_HEADS, HEAD_DIM, HEAD_DIM), f32)
        n0 = jnp.zeros((bsz, C_HEADS, HEAD_DIM), f32)
        m0 = jnp.zeros((bsz, C_HEADS), f32)
    else:
        c0, n0, m0 = st['c_C'].astype(f32), st['c_n'].astype(f32), st['c_m'].astype(f32)
    hc, c1, n1, m1 = mlstm_scan(qc, kc, vc, ig, lf, c0, n0, m0)
    hc = hc * lax.rsqrt(jnp.mean(hc * hc, axis=-1, keepdims=True) + EPS)
    o_c = (jax.nn.sigmoid(c_o.astype(f32)) * hc.reshape(bsz, t, C_W) * w['c_norm'].astype(f32)).astype(xn.dtype)

    conv_buf = jnp.zeros((bsz, D_CONV - 1, D_CONV_CH), xn.dtype) if st is None else st['d_conv']
    xbc, conv_new = causal_dwconv(d_xbc, conv_buf, w['d_conv_w'], w['d_conv_b'])
    xbc = jax.nn.silu(xbc)
    dx, dbm, dcm = jnp.split(xbc, [D_W, D_W + D_BC], axis=-1)
    rep = D_HEADS // D_GROUPS
    dx = dx.reshape(bsz, t, D_HEADS, HEAD_DIM).astype(f32)
    dbm = jnp.repeat(dbm.reshape(bsz, t, D_GROUPS, D_STATE), rep, axis=2).astype(f32)
    dcm = jnp.repeat(dcm.reshape(bsz, t, D_GROUPS, D_STATE), rep, axis=2).astype(f32)
    dt = jax.nn.softplus((d_dt + w['d_dt_bias']).astype(f32))
    a_neg = -jnp.exp(w['d_A_log'].astype(f32))
    h0 = jnp.zeros((bsz, D_HEADS, HEAD_DIM, D_STATE), f32) if st is None else st['d_ssm'].astype(f32)
    yd, h1 = ssd_scan(dx, dt, a_neg, dbm, dcm, h0)
    yd = yd + w['d_D'].astype(f32)[:, None] * dx
    o_d = rms_norm(yd.reshape(bsz, t, D_W) * jax.nn.silu(d_z.astype(f32)), w['d_norm']).astype(xn.dtype)

    g = jax.nn.sigmoid(gate.reshape(bsz, t, N_BRANCH, D_MODEL))
    merged = (g[:, :, 0] * (o_a @ w['w_br_a']) + g[:, :, 1] * (o_b @ w['w_br_b'])
              + g[:, :, 2] * (o_c @ w['w_br_c']) + g[:, :, 3] * (o_d @ w['w_br_d']))
    return merged @ w['w_out'], (ka, va, ik, kb, vb, c1, n1, m1, conv_new, h1)


def layer(x, pos, w, st):
    mix, new_state = token_mix(rms_norm(x, w['norm_mix_pre']), pos, w, st)
    x = x + rms_norm(mix, w['norm_mix_post'])
    u = rms_norm(x, w['norm_ffn_pre']) @ w['ffn_up']
    buf = jnp.zeros((x.shape[0], FFN_CONV - 1, 2 * D_FF), x.dtype) if st is None else st['ffn_conv']
    u, ffn_buf = causal_dwconv(u, buf, w['ffn_conv_w'], w['ffn_conv_b'])
    val, gte = jnp.split(u, 2, axis=-1)
    f = (val * jax.nn.gelu(gte, approximate=True)) @ w['ffn_down']
    x = x + rms_norm(f, w['norm_ffn_post'])
    return x, new_state + (ffn_buf,)


def setup_inputs(seed: int = 0) -> dict:
    key = jax.random.key(seed)
    keys = jax.random.split(key, 48)
    counter = [0]

    def nk():
        counter[0] += 1
        return keys[counter[0] - 1]

    def nrm(shape, scale=1.0):
        return scale * jax.random.normal(nk(), shape, f32)

    n_pages = PAST_LEN // PAGE_SIZE
    n_used = DEC_BATCH * n_pages
    n_pool = (5 * n_used + 3) // 4
    page_table = jax.random.permutation(nk(), n_pool)[:n_used].reshape(DEC_BATCH, n_pages).astype(jnp.int32)
    dt0 = jnp.exp(jax.random.uniform(nk(), (DEPTH, D_HEADS), f32, math.log(1e-3), math.log(1e-1)))
    return {
        'x_prompt': nrm((BATCH, SEQ, D_MODEL)),
        'x_sample': nrm((DEC_BATCH, DEC_SEQ, D_MODEL)),
        'cache_a_k': nrm((DEPTH, n_pool, PAGE_SIZE, A_KV_HEADS, HEAD_DIM)),
        'cache_a_v': nrm((DEPTH, n_pool, PAGE_SIZE, A_KV_HEADS, HEAD_DIM)),
        'cache_a_kidx': nrm((DEPTH, n_pool, PAGE_SIZE, IDX_DIM)),
        'cache_b_k': nrm((DEPTH, n_pool, PAGE_SIZE, B_HEADS, HEAD_DIM)),
        'cache_b_v': nrm((DEPTH, n_pool, PAGE_SIZE, B_HEADS, HEAD_DIM)),
        'state_c_C': nrm((DEPTH, DEC_BATCH, C_HEADS, HEAD_DIM, HEAD_DIM)),
        'state_c_n': nrm((DEPTH, DEC_BATCH, C_HEADS, HEAD_DIM)),
        'state_c_m': nrm((DEPTH, DEC_BATCH, C_HEADS), 0.5),
        'state_d_conv': nrm((DEPTH, DEC_BATCH, D_CONV - 1, D_CONV_CH)),
        'state_d_ssm': nrm((DEPTH, DEC_BATCH, D_HEADS, HEAD_DIM, D_STATE), 0.5),
        'state_ffn_conv': nrm((DEPTH, DEC_BATCH, FFN_CONV - 1, 2 * D_FF)),
        'page_table': page_table,
        'norm_mix_pre': 1.0 + nrm((DEPTH, D_MODEL), 0.05),
        'norm_mix_post': 1.0 + nrm((DEPTH, D_MODEL), 0.05),
        'w_in': nrm((DEPTH, D_MODEL, IN_W), D_MODEL ** -0.5),
        'c_b_i': nrm((DEPTH, C_HEADS), 0.1),
        'c_b_f': jnp.linspace(3.0, 6.0, C_HEADS, dtype=f32)[None, :] + nrm((DEPTH, C_HEADS), 0.1),
        'c_norm': 1.0 + nrm((DEPTH, C_W), 0.05),
        'd_conv_w': nrm((DEPTH, D_CONV, D_CONV_CH), D_CONV ** -0.5),
        'd_conv_b': nrm((DEPTH, D_CONV_CH), 0.02),
        'd_dt_bias': dt0 + jnp.log(-jnp.expm1(-dt0)),
        'd_A_log': jnp.log(jax.random.uniform(nk(), (DEPTH, D_HEADS), f32, 1.0, 16.0)),
        'd_D': 1.0 + nrm((DEPTH, D_HEADS), 0.1),
        'd_norm': 1.0 + nrm((DEPTH, D_W), 0.05),
        'w_br_a': nrm((DEPTH, A_Q, D_MODEL), A_Q ** -0.5),
        'w_br_b': nrm((DEPTH, B_W, D_MODEL), B_W ** -0.5),
        'w_br_c': nrm((DEPTH, C_W, D_MODEL), C_W ** -0.5),
        'w_br_d': nrm((DEPTH, D_W, D_MODEL), D_W ** -0.5),
        'w_out': nrm((DEPTH, D_MODEL, D_MODEL), D_MODEL ** -0.5),
        'norm_ffn_pre': 1.0 + nrm((DEPTH, D_MODEL), 0.05),
        'norm_ffn_post': 1.0 + nrm((DEPTH, D_MODEL), 0.05),
        'ffn_up': nrm((DEPTH, D_MODEL, 2 * D_FF), D_MODEL ** -0.5),
        'ffn_conv_w': nrm((DEPTH, FFN_CONV, 2 * D_FF), FFN_CONV ** -0.5),
        'ffn_conv_b': nrm((DEPTH, 2 * D_FF), 0.02),
        'ffn_down': nrm((DEPTH, D_FF, D_MODEL), D_FF ** -0.5),
    }


def reference(x_prompt, x_sample, cache_a_k, cache_a_v, cache_a_kidx, cache_b_k, cache_b_v,
              state_c_C, state_c_n, state_c_m, state_d_conv, state_d_ssm, state_ffn_conv, page_table,
              norm_mix_pre, norm_mix_post, w_in, c_b_i, c_b_f, c_norm, d_conv_w, d_conv_b, d_dt_bias,
              d_A_log, d_D, d_norm, w_br_a, w_br_b, w_br_c, w_br_d, w_out, norm_ffn_pre, norm_ffn_post,
              ffn_up, ffn_conv_w, ffn_conv_b, ffn_down):
    past = page_table.shape[1] * cache_a_k.shape[2]
    pos_p = jnp.arange(x_prompt.shape[1], dtype=jnp.int32)
    pos_s = past + jnp.arange(x_sample.shape[1], dtype=jnp.int32)
    xp, xs = x_prompt, x_sample
    p_list, s_list = [], []
    for l in range(DEPTH):
        w = {'norm_mix_pre': norm_mix_pre[l], 'norm_mix_post': norm_mix_post[l], 'w_in': w_in[l],
             'c_b_i': c_b_i[l], 'c_b_f': c_b_f[l], 'c_norm': c_norm[l], 'd_conv_w': d_conv_w[l],
             'd_conv_b': d_conv_b[l], 'd_dt_bias': d_dt_bias[l], 'd_A_log': d_A_log[l], 'd_D': d_D[l],
             'd_norm': d_norm[l], 'w_br_a': w_br_a[l], 'w_br_b': w_br_b[l], 'w_br_c': w_br_c[l],
             'w_br_d': w_br_d[l], 'w_out': w_out[l], 'norm_ffn_pre': norm_ffn_pre[l],
             'norm_ffn_post': norm_ffn_post[l], 'ffn_up': ffn_up[l], 'ffn_conv_w': ffn_conv_w[l],
             'ffn_conv_b': ffn_conv_b[l], 'ffn_down': ffn_down[l]}
        st = {'page_table': page_table, 'a_k': cache_a_k[l], 'a_v': cache_a_v[l], 'a_kidx': cache_a_kidx[l],
              'b_k': cache_b_k[l], 'b_v': cache_b_v[l], 'c_C': state_c_C[l], 'c_n': state_c_n[l],
              'c_m': state_c_m[l], 'd_conv': state_d_conv[l], 'd_ssm': state_d_ssm[l],
              'ffn_conv': state_ffn_conv[l]}
        xp, sp = layer(xp, pos_p, w, None)
        xs, ss = layer(xs, pos_s, w, st)
        p_list.append(sp)
        s_list.append(ss)
    p_st = [jnp.stack([sl[i] for sl in p_list]) for i in range(11)]
    s_st = [jnp.stack([sl[i] for sl in s_list]) for i in range(11)]
    (p_a_k, p_a_v, p_a_kidx, p_b_k, p_b_v, p_c_C, p_c_n, p_c_m, p_d_conv, p_d_ssm, p_ffn_conv) = p_st
    (s_a_k, s_a_v, s_a_kidx, s_b_k, s_b_v, s_c_C, s_c_n, s_c_m, s_d_conv, s_d_ssm, s_ffn_conv) = s_st
    return (xp, xs, p_a_k, p_a_v, p_a_kidx, p_b_k, p_b_v, p_c_C, p_c_n, p_c_m, p_d_conv, p_d_ssm, p_ffn_conv,
            s_a_k, s_a_v, s_a_kidx, s_b_k, s_b_v, s_c_C, s_c_n, s_c_m, s_d_conv, s_d_ssm, s_ffn_conv)
```

```python
import functools
import math

import jax
import jax.numpy as jnp
import numpy as np
from jax import lax
from jax.experimental import pallas as pl
from jax.experimental.pallas import tpu as pltpu

f32 = jnp.float32
bf16 = jnp.bfloat16
i32 = jnp.int32

D_MODEL = 1024
HEAD_DIM = 64
N_HEADS = 4
MIX_W = N_HEADS * HEAD_DIM
D_STATE = 128
D_GROUPS = 2
D_CONV = 4
D_CONV_CH = MIX_W + 2 * D_GROUPS * D_STATE
D_FF = 2816
FFN_CONV = 3
TOPK_MAX = 256
ROPE_THETA = 10000.0
EPS = 1e-6
NEG = -1e30
LANES = 128
SUBLANES = 8
CHUNK = 128
KV_BLOCK = 256
VMEM_LIMIT = 56 * 1024 * 1024

P_QA, P_KA, P_VA, P_IQ, P_IK = 0, 256, 512, 768, 1024
P_QB, P_KB, P_VB = 1280, 1536, 1792
P_QC, P_KC, P_VC, P_CO = 2048, 2304, 2560, 2816
P_DZ, P_DXBC, P_SMALL = 3072, 3328, 4096
P_W = 4224
S_IW, S_CI, S_CF, S_DT = 0, 4, 8, 12


def _nt(a, b):
    return lax.dot_general(a, b, (((1,), (1,)), ((), ())), preferred_element_type=f32)


def _tn(a, b):
    return lax.dot_general(a, b, (((0,), (0,)), ((), ())), preferred_element_type=f32)


def _dot(a, b):
    return jnp.dot(a, b, preferred_element_type=f32)


def _split3(x):
    h = x.astype(bf16)
    r = x - h.astype(f32)
    m = r.astype(bf16)
    l = (r - m.astype(f32)).astype(bf16)
    return h, m, l


def _dot_exact_lhs(t01, x):
    h, m, l = _split3(x)
    return _dot(t01, h) + _dot(t01, m) + _dot(t01, l)


def _head_lane_mask(shape, h, width=HEAD_DIM):
    lane = lax.broadcasted_iota(i32, shape, len(shape) - 1)
    return (lane >= h * width) & (lane < (h + 1) * width)


def _expand_heads(cols, rows):
    out = jnp.broadcast_to(cols[3], (rows, MIX_W))
    for h in (2, 1, 0):
        out = jnp.where(_head_lane_mask((rows, MIX_W), h), jnp.broadcast_to(cols[h], (rows, MIX_W)), out)
    return out


def _rms(x, g):
    return x * lax.rsqrt(jnp.mean(x * x, axis=-1, keepdims=True) + EPS) * g


def _rope_seg(v, cs):
    c, s = cs[:, :LANES], cs[:, LANES:]
    lane = lax.broadcasted_iota(i32, (v.shape[0], LANES), 1)
    first = (lane & 32) == 0
    outs = []
    for j in range(2):
        u = v[:, j * LANES:(j + 1) * LANES]
        sw = jnp.where(first, pltpu.roll(u, LANES - 32, 1), pltpu.roll(u, 32, 1))
        outs.append(u * c + sw * s)
    return jnp.concatenate(outs, axis=1)


def _proj_kernel(x_ref, g_ref, w_ref, cs_ref,
                 qa_ref, ka_ref, karep_ref, va_ref, varep_ref, iq_ref, ik_ref, ik4_ref,
                 qb_ref, kb_ref, kbh_ref, vb_ref, vbh_ref,
                 qc_ref, kc_ref, vc_ref, co_ref, dz_ref, dxbc_ref, small_ref):
    xn = _rms(x_ref[...], g_ref[...]).astype(bf16)
    cs = cs_ref[...]
    tm = xn.shape[0]

    def seg(off, width=MIX_W):
        return _dot(xn, w_ref[:, off:off + width])

    lane = lax.broadcasted_iota(i32, (tm, LANES), 1)
    qa_ref[...] = _rope_seg(seg(P_QA), cs).astype(bf16)
    ka = _rope_seg(seg(P_KA), cs)
    karep_ref[...] = ka.astype(bf16)
    ka_ref[...] = jnp.where(lane < HEAD_DIM, ka[:, :LANES], ka[:, LANES:])
    va = seg(P_VA)
    varep_ref[...] = va.astype(bf16)
    va_ref[...] = jnp.where(lane < HEAD_DIM, va[:, :LANES], va[:, LANES:])
    iq_ref[...] = _rope_seg(seg(P_IQ), cs).astype(bf16)
    ik4 = _rope_seg(seg(P_IK), cs)
    ik4_ref[...] = ik4.astype(bf16)
    ik_ref[...] = ik4[:, :HEAD_DIM]
    qb_ref[...] = seg(P_QB).astype(bf16)
    kb = seg(P_KB)
    kb_ref[...] = kb
    kbh_ref[...] = kb.astype(bf16)
    vb = seg(P_VB)
    vb_ref[...] = vb
    vbh_ref[...] = vb.astype(bf16)
    qc_ref[...] = seg(P_QC).astype(bf16)
    kc_ref[...] = seg(P_KC).astype(bf16)
    vc_ref[...] = seg(P_VC).astype(bf16)
    co_ref[...] = seg(P_CO)
    dz_ref[...] = seg(P_DZ)
    for j in range(3):
        dxbc_ref[:, j * MIX_W:(j + 1) * MIX_W] = seg(P_DXBC + j * MIX_W)
    small_ref[...] = seg(P_SMALL, LANES)


def _proj(x, g, w, cs, tm):
    n = x.shape[0]
    nt = n // tm
    period = cs.shape[0] // tm
    row = lambda i: (i, 0)
    const = lambda i: (0, 0)

    def o(width, dt):
        return jax.ShapeDtypeStruct((n, width), dt), pl.BlockSpec((tm, width), row)

    outs = [o(256, bf16), o(128, f32), o(256, bf16), o(128, f32), o(256, bf16), o(256, bf16), o(64, f32), o(256, bf16),
            o(256, bf16), o(256, f32), o(256, bf16), o(256, f32), o(256, bf16),
            o(256, bf16), o(256, bf16), o(256, bf16), o(256, f32), o(256, f32), o(768, f32), o(128, f32)]
    return pl.pallas_call(
        _proj_kernel,
        grid=(nt,),
        in_specs=[pl.BlockSpec((tm, D_MODEL), row), _resident((1, D_MODEL)),
                  _resident((D_MODEL, P_W)), pl.BlockSpec((tm, 2 * LANES), lambda i: (i % period, 0))],
        out_specs=[s for _, s in outs],
        out_shape=[s for s, _ in outs],
        compiler_params=pltpu.CompilerParams(dimension_semantics=("arbitrary",), vmem_limit_bytes=VMEM_LIMIT),
        name="proj",
    )(x, g, w, cs)


def _stack_heads(q):
    return jnp.concatenate([jnp.where(_head_lane_mask(q.shape, h), q, jnp.zeros_like(q)) for h in range(N_HEADS)], axis=0)


def _sb_chunk(qm, kc, vc, uo, tq, mask, r_ref, oacc_ref):
    z = _nt(qm, kc)
    lk = -(jnp.maximum(z, 0.0) + jnp.log(1.0 + jnp.exp(-jnp.abs(z))))
    if mask is not None:
        lk = jnp.where(mask, lk, 0.0)
    halves = [None, None]
    for half in (1, 0):
        sl = slice(half * LANES, (half + 1) * LANES)
        lkh = lk[:, sl]
        hi = lkh.astype(bf16)
        lo = (lkh - hi.astype(f32)).astype(bf16)
        res = _dot(jnp.concatenate([hi, lo], axis=1), uo)
        r = r_ref[...]
        a = jnp.exp(z[:, sl] + lkh + res[:, :LANES] + r)
        if mask is not None:
            a = jnp.where(mask[:, sl], a, 0.0)
        r_ref[...] = r + res[:, LANES:]
        halves[half] = a.astype(bf16)
    a = jnp.concatenate(halves, axis=1)
    acc = oacc_ref[...]
    for h in range(N_HEADS):
        vh = jnp.where(_head_lane_mask(vc.shape, h), vc, jnp.zeros_like(vc))
        acc = acc + _dot(a[h * tq:(h + 1) * tq], vh)
    oacc_ref[...] = acc


def _sb_kernel(q_ref, k_ref, v_ref, uo_ref, o_ref, oacc_ref, r_ref):
    tq = q_ref.shape[1]
    i = pl.program_id(1)
    qm = _stack_heads(q_ref[0])
    uo = uo_ref[...]
    oacc_ref[...] = jnp.zeros_like(oacc_ref)
    r_ref[...] = jnp.zeros_like(r_ref)
    rows = lax.broadcasted_iota(i32, (N_HEADS * tq, KV_BLOCK), 0) & (tq - 1)
    cols = lax.broadcasted_iota(i32, (N_HEADS * tq, KV_BLOCK), 1)

    def blk(c):
        start = pl.multiple_of(c * KV_BLOCK, KV_BLOCK)
        return k_ref[0, pl.ds(start, KV_BLOCK), :], v_ref[0, pl.ds(start, KV_BLOCK), :]

    kc, vc = blk(i)
    _sb_chunk(qm, kc, vc, uo, tq, cols < rows, r_ref, oacc_ref)

    def body(j, carry):
        kc, vc = blk(i - 1 - j)
        _sb_chunk(qm, kc, vc, uo, tq, None, r_ref, oacc_ref)
        return carry

    lax.fori_loop(0, i, body, 0)
    o_ref[0] = oacc_ref[...].astype(o_ref.dtype)


def _suffix_ones():
    j = np.arange(2 * LANES)[:, None] % LANES
    s = np.arange(2 * LANES)[None, :]
    return jnp.asarray(np.where(s < LANES, (j > s), True).astype(np.float32), dtype=bf16)


def _stickbreak_prompt(q, k, v):
    b, t, _ = q.shape
    tq = KV_BLOCK
    return pl.pallas_call(
        _sb_kernel,
        grid=(b, t // tq),
        in_specs=[pl.BlockSpec((1, tq, MIX_W), lambda bb, i: (bb, i, 0)),
                  pl.BlockSpec((1, t, MIX_W), lambda bb, i: (bb, 0, 0)),
                  pl.BlockSpec((1, t, MIX_W), lambda bb, i: (bb, 0, 0)),
                  pl.BlockSpec((2 * LANES, 2 * LANES), lambda bb, i: (0, 0))],
        out_specs=pl.BlockSpec((1, tq, MIX_W), lambda bb, i: (bb, i, 0)),
        out_shape=jax.ShapeDtypeStruct((b, t, MIX_W), bf16),
        scratch_shapes=[pltpu.VMEM((tq, MIX_W), f32), pltpu.VMEM((N_HEADS * tq, LANES), f32)],
        compiler_params=pltpu.CompilerParams(dimension_semantics=("arbitrary", "arbitrary"), vmem_limit_bytes=VMEM_LIMIT),
        name="stickbreak",
    )(q, k, v, _suffix_ones())


INT_MIN = -2147483648
IDX_BIG = 1 << 30


def _tile2(v):
    return jnp.concatenate([v, v], axis=1)


def _tile_heads(v):
    return jnp.concatenate([v] * N_HEADS, axis=0)


def _key_to_float(key):
    bits = jnp.where(key >= 0, key, (-key) | jnp.int32(INT_MIN))
    return lax.bitcast_convert_type(bits, f32)


def _count(mask01, ones):
    return _dot(mask01.astype(bf16), ones)


def _select_topk(sc_ref, nc, n_skip, topk, kidx0, ones, j_ref, idx_bits):
    tq = sc_ref.shape[1]
    n_skip_f = n_skip.astype(f32)
    kf = jnp.float32(topk)

    def count_ge(candf):
        c2 = _tile2(candf)

        def body(c, acc):
            return acc + _count(jnp.where(sc_ref[c] >= c2, 1.0, 0.0), ones)

        cnt = lax.fori_loop(0, nc, body, jnp.zeros((tq, LANES), f32))
        return cnt + jnp.where(jnp.float32(NEG) >= candf, n_skip_f, 0.0)

    def bit_step(it, prefix):
        cand = prefix + jnp.left_shift(jnp.int32(1), 31 - it)
        return jnp.where(count_ge(_key_to_float(cand)) >= kf, cand, prefix)

    prefix = lax.fori_loop(0, 32, bit_step, jnp.full((tq, LANES), INT_MIN, i32))
    thr = _key_to_float(prefix)
    thr2 = _tile2(thr)

    def gt_eq(c, acc):
        s = sc_ref[c]
        m = jnp.concatenate([jnp.where(s > thr2, 1.0, 0.0), jnp.where(s == thr2, 1.0, 0.0)], axis=0)
        return acc + _count(m, ones)

    cnt = lax.fori_loop(0, nc, gt_eq, jnp.zeros((2 * tq, LANES), f32))
    n_gt = cnt[:tq] + jnp.where(jnp.float32(NEG) > thr, n_skip_f, 0.0)
    n_eq = cnt[tq:]
    room = kf - n_gt
    j_ref[...] = jnp.full((tq, LANES), IDX_BIG, i32)
    need = jnp.max(jnp.where(n_eq > room, 1.0, 0.0)) > 0.0

    @pl.when(need)
    def _():
        lane = lax.broadcasted_iota(i32, (tq, KV_BLOCK), 1)

        def count_lt(candj):
            c2 = _tile2(candj)

            def body(c, acc):
                kidx = kidx0 + c * KV_BLOCK + lane
                return acc + _count(jnp.where((sc_ref[c] == thr2) & (kidx < c2), 1.0, 0.0), ones)

            return lax.fori_loop(0, nc, body, jnp.zeros((tq, LANES), f32))

        def jbit(it, j):
            cand = j | jnp.left_shift(jnp.int32(1), idx_bits - 1 - it)
            return jnp.where(count_lt(cand) <= room, cand, j)

        j_ref[...] = lax.fori_loop(0, idx_bits, jbit, jnp.zeros((tq, LANES), i32))

    return thr


def _dsa_kernel(iq_ref, small_ref, qa_ref, ik_ref, k_ref, v_ref, ones_ref, o_ref,
                sc_ref, j_ref, m_ref, l_ref, acc_ref, *, topk):
    tq = iq_ref.shape[1]
    t = ik_ref.shape[1]
    i = pl.program_id(1)
    nc = i + 1
    n_skip = t - nc * KV_BLOCK
    ones = ones_ref[...]
    lane = lax.broadcasted_iota(i32, (tq, KV_BLOCK), 1)
    qpos = i * tq + lax.broadcasted_iota(i32, (tq, KV_BLOCK), 0)
    iqm = _stack_heads(iq_ref[0])
    iw = small_ref[0][:, S_IW:S_IW + N_HEADS]

    def blk(ref, c):
        return ref[0, pl.ds(pl.multiple_of(c * KV_BLOCK, KV_BLOCK), KV_BLOCK), :]

    def score_blk(c, carry):
        s = _nt(iqm, blk(ik_ref, c))
        sc = iw[:, 0:1] * jnp.maximum(s[:tq], 0.0)
        for h in range(1, N_HEADS):
            sc = sc + iw[:, h:h + 1] * jnp.maximum(s[h * tq:(h + 1) * tq], 0.0)
        sc_ref[c] = jnp.where(c * KV_BLOCK + lane <= qpos, sc, NEG)
        return carry

    lax.fori_loop(0, nc, score_blk, 0)
    idx_bits = max(1, int(math.ceil(math.log2(t)))) + 1
    thr = _select_topk(sc_ref, nc, n_skip, topk, 0, ones, j_ref, idx_bits)
    thr2, j2 = _tile2(thr), _tile2(j_ref[...])

    qm = _stack_heads(qa_ref[0])
    m_ref[...] = jnp.full_like(m_ref, NEG)
    l_ref[...] = jnp.zeros_like(l_ref)
    acc_ref[...] = jnp.zeros_like(acc_ref)

    def attn_blk(c, carry):
        s = sc_ref[c]
        kidx = c * KV_BLOCK + lane
        sel = ((s > thr2) | ((s == thr2) & (kidx < j2))) & (kidx <= qpos)
        sel4 = _tile_heads(jnp.where(sel, 1.0, 0.0)) > 0.5
        z = jnp.where(sel4, _nt(qm, blk(k_ref, c)), NEG)
        m_old = m_ref[...]
        m_new = jnp.maximum(m_old, jnp.max(z, axis=1, keepdims=True))
        p = jnp.where(sel4, jnp.exp(z - m_new), 0.0)
        alpha = jnp.exp(m_old - m_new)
        l_ref[...] = alpha * l_ref[...] + jnp.sum(p, axis=1, keepdims=True)
        m_ref[...] = m_new
        pb = p.astype(bf16)
        vc = blk(v_ref, c)
        acc = acc_ref[...] * _expand_heads([alpha[h * tq:(h + 1) * tq] for h in range(N_HEADS)], tq)
        for h in range(N_HEADS):
            vh = jnp.where(_head_lane_mask(vc.shape, h), vc, jnp.zeros_like(vc))
            acc = acc + _dot(pb[h * tq:(h + 1) * tq], vh)
        acc_ref[...] = acc
        return carry

    lax.fori_loop(0, nc, attn_blk, 0)
    inv = 1.0 / l_ref[...]
    o_ref[0] = (acc_ref[...] * _expand_heads([inv[h * tq:(h + 1) * tq] for h in range(N_HEADS)], tq)).astype(o_ref.dtype)


def _dsa_prompt(iq, small, qa, ik4, karep, varep):
    b, t, _ = iq.shape
    tq = KV_BLOCK
    topk = min(TOPK_MAX, t // 4)
    qblk = lambda bb, i: (bb, i, 0)
    whole = lambda bb, i: (bb, 0, 0)
    return pl.pallas_call(
        functools.partial(_dsa_kernel, topk=topk),
        grid=(b, t // tq),
        in_specs=[pl.BlockSpec((1, tq, MIX_W), qblk), pl.BlockSpec((1, tq, LANES), qblk), pl.BlockSpec((1, tq, MIX_W), qblk),
                  pl.BlockSpec((1, t, MIX_W), whole), pl.BlockSpec((1, t, MIX_W), whole), pl.BlockSpec((1, t, MIX_W), whole),
                  pl.BlockSpec((KV_BLOCK, LANES), lambda bb, i: (0, 0))],
        out_specs=pl.BlockSpec((1, tq, MIX_W), qblk),
        out_shape=jax.ShapeDtypeStruct((b, t, MIX_W), bf16),
        scratch_shapes=[pltpu.VMEM((t // KV_BLOCK, tq, KV_BLOCK), f32), pltpu.VMEM((tq, LANES), i32),
                        pltpu.VMEM((N_HEADS * tq, 1), f32), pltpu.VMEM((N_HEADS * tq, 1), f32), pltpu.VMEM((tq, MIX_W), f32)],
        compiler_params=pltpu.CompilerParams(dimension_semantics=("arbitrary", "arbitrary"), vmem_limit_bytes=VMEM_LIMIT),
        name="dsa",
    )(iq, small, qa, ik4, karep, varep, jnp.ones((KV_BLOCK, LANES), bf16))


C_EXT = MIX_W + LANES


def _log_sigmoid(x):
    return jnp.minimum(x, 0.0) - jnp.log(1.0 + jnp.exp(-jnp.abs(x)))


def _lower_tri():
    r = np.arange(CHUNK)
    return jnp.asarray((r[None, :] <= r[:, None]).astype(np.float32), dtype=bf16)


def _mlstm_kernel(q_ref, k_ref, v_ref, co_ref, small_ref, bias_ref, cnorm_ref, ltri_ref, s0_ref, m0_ref,
                  o_ref, s_ref, m_ref):
    c = CHUNK
    i = pl.program_id(1)

    @pl.when(i == 0)
    def _():
        s_ref[...] = s0_ref[...]
        m_ref[...] = m0_ref[...]

    lane = lax.broadcasted_iota(i32, (c, LANES), 1)
    g0 = small_ref[0] + bias_ref[...]
    g = jnp.where((lane >= S_CF) & (lane < S_CF + N_HEADS), _log_sigmoid(g0), g0)
    bc = _dot_exact_lhs(ltri_ref[...], g)
    gt, bt = g.T, bc.T
    q, k, v = q_ref[0], k_ref[0], v_ref[0]
    s_all = _nt(_stack_heads(q), k)
    s_ext = s_ref[0]
    m_all = m_ref[0]
    inter = _dot(q, s_ext.astype(bf16))
    tpos = lax.broadcasted_iota(i32, (c, c), 0)
    spos = lax.broadcasted_iota(i32, (c, c), 1)
    num = jnp.zeros((c, MIX_W), f32)
    w_cols, hd_cols, u_cols, dec_rows, m_new_all = [], [], [], [], m_all
    for h in range(N_HEADS):
        b_col, b_row = bc[:, S_CF + h:S_CF + h + 1], bt[S_CF + h:S_CF + h + 1, :]
        i_col, i_row = g[:, S_CI + h:S_CI + h + 1], gt[S_CI + h:S_CI + h + 1, :]
        m_h = m_all[0:1, h:h + 1]
        logw = jnp.where(spos <= tpos, b_col - b_row + i_row, -jnp.inf)
        log_inter = b_col + m_h
        mt = jnp.maximum(log_inter, jnp.max(logw, axis=1, keepdims=True))
        sc = s_all[h * c:(h + 1) * c] * jnp.exp(logw - mt)
        w_inter = jnp.exp(log_inter - mt)
        den = jnp.sum(sc, axis=1, keepdims=True) + w_inter * inter[:, MIX_W + h:MIX_W + h + 1]
        vh = jnp.where(_head_lane_mask(v.shape, h), v, jnp.zeros_like(v))
        num = num + _dot(sc.astype(bf16), vh)
        w_cols.append(w_inter)
        hd_cols.append(1.0 / jnp.maximum(jnp.abs(den), jnp.exp(-mt)))
        b_end = bc[c - 1:c, S_CF + h:S_CF + h + 1]
        m_new = jnp.maximum(b_end + m_h, jnp.max(b_end - b_row + i_row, axis=1, keepdims=True))
        u_cols.append(jnp.exp(b_end - b_col + i_col - m_new))
        dec_rows.append(jnp.exp(b_end + m_h - m_new))
        hl = lax.broadcasted_iota(i32, m_all.shape, 1)
        m_new_all = jnp.where(hl == h, m_new, m_new_all)
    hc = (num + _expand_heads(w_cols, c) * inter[:, :MIX_W]) * _expand_heads(hd_cols, c)
    sq = hc * hc
    rs = [lax.rsqrt(jnp.sum(jnp.where(_head_lane_mask(sq.shape, h), sq, 0.0), axis=1, keepdims=True) * (1.0 / HEAD_DIM) + EPS)
          for h in range(N_HEADS)]
    o_ref[0] = (jax.nn.sigmoid(co_ref[0]) * (hc * _expand_heads(rs, c)) * cnorm_ref[...]).astype(o_ref.dtype)

    ku = (k.astype(f32) * _expand_heads(u_cols, c)).astype(bf16)
    upd = _tn(ku, jnp.concatenate([v, jnp.ones((c, LANES), bf16)], axis=1))
    rh = lax.broadcasted_iota(i32, (MIX_W, C_EXT), 0) // HEAD_DIM
    cl = lax.broadcasted_iota(i32, (MIX_W, C_EXT), 1)
    struct = jnp.where(cl < MIX_W, cl // HEAD_DIM, cl - MIX_W) == rh
    dec = jnp.broadcast_to(dec_rows[3], (MIX_W, C_EXT))
    for h in (2, 1, 0):
        dec = jnp.where(rh == h, jnp.broadcast_to(dec_rows[h], (MIX_W, C_EXT)), dec)
    s_ref[0] = dec * s_ext + jnp.where(struct, upd, 0.0)
    m_ref[0] = m_new_all


def _mlstm_prompt(q, k, v, co, small, bias_row, cnorm, s0, m0):
    b, t, _ = q.shape
    blk = lambda bb, i: (bb, i, 0)
    st = lambda bb, i: (bb, 0, 0)
    const = lambda bb, i: (0, 0)
    return pl.pallas_call(
        _mlstm_kernel,
        grid=(b, t // CHUNK),
        in_specs=[pl.BlockSpec((1, CHUNK, MIX_W), blk)] * 4 + [pl.BlockSpec((1, CHUNK, LANES), blk),
                  pl.BlockSpec((1, LANES), const), pl.BlockSpec((1, MIX_W), const), pl.BlockSpec((CHUNK, CHUNK), const),
                  pl.BlockSpec((1, MIX_W, C_EXT), st), pl.BlockSpec((1, SUBLANES, LANES), st)],
        out_specs=[pl.BlockSpec((1, CHUNK, MIX_W), blk), pl.BlockSpec((1, MIX_W, C_EXT), st), pl.BlockSpec((1, SUBLANES, LANES), st)],
        out_shape=[jax.ShapeDtypeStruct((b, t, MIX_W), bf16), jax.ShapeDtypeStruct((b, MIX_W, C_EXT), f32),
                   jax.ShapeDtypeStruct((b, SUBLANES, LANES), f32)],
        compiler_params=pltpu.CompilerParams(dimension_semantics=("arbitrary", "arbitrary"), vmem_limit_bytes=VMEM_LIMIT),
        name="mlstm",
    )(q, k, v, co, small, bias_row, cnorm, _lower_tri(), s0, m0)


def _mlstm_state_to_ext(c0, n0, m0):
    b = c0.shape[0]
    eye = jnp.eye(N_HEADS, dtype=f32)
    blocks = jnp.einsum("bhed,hg->bhdge", c0, eye).reshape(b, MIX_W, MIX_W)
    ncols = jnp.einsum("bhd,hg->bhdg", n0, eye).reshape(b, MIX_W, N_HEADS)
    s_ext = jnp.concatenate([blocks, ncols, jnp.zeros((b, MIX_W, LANES - N_HEADS), f32)], axis=-1)
    m = jnp.zeros((b, SUBLANES, LANES), f32).at[:, 0, :N_HEADS].set(m0)
    return s_ext, m


def _mlstm_state_from_ext(s_ext, m):
    b = s_ext.shape[0]
    blocks = s_ext[:, :, :MIX_W].reshape(b, N_HEADS, HEAD_DIM, N_HEADS, HEAD_DIM)
    c1 = jnp.stack([blocks[:, h, :, h, :] for h in range(N_HEADS)], axis=1)
    ncols = s_ext[:, :, MIX_W:MIX_W + N_HEADS].reshape(b, N_HEADS, HEAD_DIM, N_HEADS)
    n1 = jnp.stack([ncols[:, h, :, h] for h in range(N_HEADS)], axis=1)
    return jnp.swapaxes(c1, 2, 3), n1, m[:, 0, :N_HEADS]


def _ssd_state_to_ext(h0):
    b = h0.shape[0]
    return h0.reshape(b, D_GROUPS, 2, HEAD_DIM, D_STATE).transpose(0, 1, 4, 2, 3).reshape(b, D_GROUPS, D_STATE, LANES)


def _ssd_state_from_ext(ht):
    b = ht.shape[0]
    return ht.reshape(b, D_GROUPS, D_STATE, 2, HEAD_DIM).transpose(0, 1, 3, 4, 2).reshape(b, N_HEADS, HEAD_DIM, D_STATE)


def _expand2(c0, c1, rows):
    lane = lax.broadcasted_iota(i32, (rows, LANES), 1)
    return jnp.where(lane < HEAD_DIM, jnp.broadcast_to(c0, (rows, LANES)), jnp.broadcast_to(c1, (rows, LANES)))


def _softplus(x):
    return jnp.maximum(x, 0.0) + jnp.log(1.0 + jnp.exp(-jnp.abs(x)))


def _ssd_kernel(x_ref, halo_ref, conv0_ref, dz_ref, small_ref, convw_ref, dtb_ref, alog_ref, dd_ref, dnorm_ref, ltri_ref, h0_ref,
                o_ref, ht_ref, xcat_ref):
    c = CHUNK
    i = pl.program_id(1)

    @pl.when(i == 0)
    def _():
        ht_ref[...] = h0_ref[...]
        xcat_ref[0:SUBLANES, :] = conv0_ref[0]

    @pl.when(i > 0)
    def _():
        xcat_ref[0:SUBLANES, :] = halo_ref[0]

    xcat_ref[SUBLANES:SUBLANES + c, :] = x_ref[0]
    w = convw_ref[...]
    y = w[D_CONV:D_CONV + 1, :] + w[0:1, :] * xcat_ref[pl.ds(SUBLANES - 3, c), :]
    for j in range(1, D_CONV):
        y = y + w[j:j + 1, :] * xcat_ref[pl.ds(SUBLANES - 3 + j, c), :]
    xbc = y * jax.nn.sigmoid(y)
    dx, dbm, dcm = xbc[:, :MIX_W], xbc[:, MIX_W:2 * MIX_W], xbc[:, 2 * MIX_W:]
    dxb = dx.astype(bf16)

    dt = _softplus(small_ref[0] + dtb_ref[...])
    a = dt * (-jnp.exp(alog_ref[...]))
    ac = _dot_exact_lhs(ltri_ref[...], a)
    dtt, act = dt.T, ac.T
    tpos = lax.broadcasted_iota(i32, (c, c), 0)
    spos = lax.broadcasted_iota(i32, (c, c), 1)
    y_all = jnp.zeros((c, MIX_W), f32)
    y_inter, new_state = [], []
    for g in range(D_GROUPS):
        bg = dbm[:, g * D_STATE:(g + 1) * D_STATE].astype(bf16)
        cg = dcm[:, g * D_STATE:(g + 1) * D_STATE].astype(bf16)
        cb = _nt(cg, bg)
        e_cols, w_cols, d_ends = [], [], []
        for h in (2 * g, 2 * g + 1):
            a_col, a_row = ac[:, S_DT + h:S_DT + h + 1], act[S_DT + h:S_DT + h + 1, :]
            dt_col, dt_row = dt[:, S_DT + h:S_DT + h + 1], dtt[S_DT + h:S_DT + h + 1, :]
            decay = jnp.exp(jnp.where(spos <= tpos, a_col - a_row, -jnp.inf))
            xh = jnp.where(_head_lane_mask(dxb.shape, h), dxb, jnp.zeros_like(dxb))
            y_all = y_all + _dot((cb * decay * dt_row).astype(bf16), xh)
            a_end = ac[c - 1:c, S_DT + h:S_DT + h + 1]
            e_cols.append(jnp.exp(a_col))
            w_cols.append(jnp.exp(a_end - a_col) * dt_col)
            d_ends.append(jnp.exp(a_end))
        ht = ht_ref[0, g]
        y_inter.append(_expand2(e_cols[0], e_cols[1], c) * _dot(cg, ht.astype(bf16)))
        xw = (dx[:, g * LANES:(g + 1) * LANES] * _expand2(w_cols[0], w_cols[1], c)).astype(bf16)
        new_state.append(_expand2(d_ends[0], d_ends[1], D_STATE) * ht + _tn(bg, xw))
    for g in range(D_GROUPS):
        ht_ref[0, g] = new_state[g]
    yd = y_all + jnp.concatenate(y_inter, axis=1) + dd_ref[...] * dx
    dz = dz_ref[0]
    o_ref[0] = _rms(yd * (dz * jax.nn.sigmoid(dz)), dnorm_ref[...]).astype(o_ref.dtype)


def _ssd_prompt(dxbc, conv0, dz, small, convw, dtb_row, alog_row, dd_row, dnorm, h0):
    b, t, _ = dxbc.shape
    blk = lambda bb, i: (bb, i, 0)
    st = lambda bb, i: (bb, 0, 0)
    const = lambda bb, i: (0, 0)
    per_chunk = CHUNK // SUBLANES
    return pl.pallas_call(
        _ssd_kernel,
        grid=(b, t // CHUNK),
        in_specs=[pl.BlockSpec((1, CHUNK, D_CONV_CH), blk),
                  pl.BlockSpec((1, SUBLANES, D_CONV_CH), lambda bb, i: (bb, jnp.maximum(i * per_chunk - 1, 0), 0)),
                  pl.BlockSpec((1, SUBLANES, D_CONV_CH), st),
                  pl.BlockSpec((1, CHUNK, MIX_W), blk), pl.BlockSpec((1, CHUNK, LANES), blk),
                  pl.BlockSpec((SUBLANES, D_CONV_CH), const), pl.BlockSpec((1, LANES), const), pl.BlockSpec((1, LANES), const),
                  pl.BlockSpec((1, MIX_W), const), pl.BlockSpec((1, MIX_W), const), pl.BlockSpec((CHUNK, CHUNK), const),
                  pl.BlockSpec((1, D_GROUPS, D_STATE, LANES), lambda bb, i: (bb, 0, 0, 0))],
        out_specs=[pl.BlockSpec((1, CHUNK, MIX_W), blk), pl.BlockSpec((1, D_GROUPS, D_STATE, LANES), lambda bb, i: (bb, 0, 0, 0))],
        out_shape=[jax.ShapeDtypeStruct((b, t, MIX_W), bf16), jax.ShapeDtypeStruct((b, D_GROUPS, D_STATE, LANES), f32)],
        scratch_shapes=[pltpu.VMEM((SUBLANES + CHUNK, D_CONV_CH), f32)],
        compiler_params=pltpu.CompilerParams(dimension_semantics=("arbitrary", "arbitrary"), vmem_limit_bytes=VMEM_LIMIT),
        name="ssd",
    )(dxbc, dxbc, conv0, dz, small, convw, dtb_row, alog_row, dd_row, dnorm, _lower_tri(), h0)


def _resident(shape):
    nd = len(shape)
    return pl.BlockSpec(shape, lambda *_: (0,) * nd, pipeline_mode=pl.Buffered(1))


def _merge_kernel(x_ref, gpre_ref, wg_ref, oa_ref, ob_ref, oc_ref, od_ref, wbr_ref, wout_ref, gpost_ref, o_ref):
    x = x_ref[...]
    xn = _rms(x, gpre_ref[...]).astype(bf16)
    merged = None
    for k, br in enumerate((oa_ref, ob_ref, oc_ref, od_ref)):
        gate = jax.nn.sigmoid(_dot(xn, wg_ref[:, k * D_MODEL:(k + 1) * D_MODEL]))
        term = gate * _dot(br[...], wbr_ref[k])
        merged = term if merged is None else merged + term
    mix = _dot(merged.astype(bf16), wout_ref[...])
    o_ref[...] = x + _rms(mix, gpost_ref[...])


def _merge(x, gpre, wg, oa, ob, oc, od, wbr, wout, gpost, tm):
    n = x.shape[0]
    row = lambda i: (i, 0)
    return pl.pallas_call(
        _merge_kernel,
        grid=(n // tm,),
        in_specs=[pl.BlockSpec((tm, D_MODEL), row), _resident((1, D_MODEL)), _resident((D_MODEL, N_HEADS * D_MODEL))]
        + [pl.BlockSpec((tm, MIX_W), row)] * 4
        + [_resident((N_HEADS, MIX_W, D_MODEL)), _resident((D_MODEL, D_MODEL)), _resident((1, D_MODEL))],
        out_specs=pl.BlockSpec((tm, D_MODEL), row),
        out_shape=jax.ShapeDtypeStruct((n, D_MODEL), f32),
        compiler_params=pltpu.CompilerParams(dimension_semantics=("arbitrary",), vmem_limit_bytes=VMEM_LIMIT),
        name="merge",
    )(x, gpre, wg, oa, ob, oc, od, wbr, wout, gpost)


FF_BLOCK = 256
FF_NBLK = D_FF // FF_BLOCK


def _gelu_tanh(x):
    return 0.5 * x * (1.0 + jnp.tanh(math.sqrt(2.0 / math.pi) * (x + 0.044715 * (x * x * x))))


def _ffn_finish(x, h_ref, down_ref, gpost_ref, o_ref):
    f = _dot(h_ref[...], down_ref[...])
    o_ref[...] = x + _rms(f, gpost_ref[...])


def _ffn_prompt_kernel(x_ref, gpre_ref, up_ref, cw_ref, down_ref, gpost_ref, o_ref, tail_ref,
                       halo_ref, ucat_ref, h_ref, *, tiles_per_seq):
    x = x_ref[...]
    tm = x.shape[0]
    xn = _rms(x, gpre_ref[...]).astype(bf16)
    first = (pl.program_id(0) % tiles_per_seq) == 0
    for j in range(FF_NBLK):
        conv = []
        for part in (j, FF_NBLK + j):
            cols = slice(part * FF_BLOCK, (part + 1) * FF_BLOCK)
            u = _dot(xn, up_ref[:, cols])
            ucat_ref[0:SUBLANES, :] = jnp.where(first, 0.0, halo_ref[part])
            ucat_ref[SUBLANES:SUBLANES + tm, :] = u
            cw = cw_ref[:, cols]
            conv.append(cw[FFN_CONV:FFN_CONV + 1] + cw[0:1] * ucat_ref[pl.ds(SUBLANES - 2, tm), :]
                        + cw[1:2] * ucat_ref[pl.ds(SUBLANES - 1, tm), :] + cw[2:3] * u)
            halo_ref[part] = u[tm - SUBLANES:, :]
            tail_ref[0, :, cols] = u[tm - SUBLANES:, :]
        h_ref[:, j * FF_BLOCK:(j + 1) * FF_BLOCK] = (conv[0] * _gelu_tanh(conv[1])).astype(bf16)
    _ffn_finish(x, h_ref, down_ref, gpost_ref, o_ref)


def _ffn_decode_kernel(x_ref, gpre_ref, up_ref, cw_ref, down_ref, gpost_ref, buf_ref, o_ref, tail_ref, h_ref):
    x = x_ref[...]
    xn = _rms(x, gpre_ref[...]).astype(bf16)
    for j in range(FF_NBLK):
        conv = []
        for part in (j, FF_NBLK + j):
            cols = slice(part * FF_BLOCK, (part + 1) * FF_BLOCK)
            u = _dot(xn, up_ref[:, cols])
            cw = cw_ref[:, cols]
            b0, b1 = buf_ref[0, :, cols], buf_ref[1, :, cols]
            conv.append(cw[FFN_CONV:FFN_CONV + 1] + cw[0:1] * b0 + cw[1:2] * b1 + cw[2:3] * u)
            tail_ref[0, :, cols] = b1
            tail_ref[1, :, cols] = u
        h_ref[:, j * FF_BLOCK:(j + 1) * FF_BLOCK] = (conv[0] * _gelu_tanh(conv[1])).astype(bf16)
    _ffn_finish(x, h_ref, down_ref, gpost_ref, o_ref)


def _ffn_prompt(x, gpre, up, cw, down, gpost, seq_len, tm):
    n = x.shape[0]
    tiles_per_seq = seq_len // tm
    row = lambda i: (i, 0)
    return pl.pallas_call(
        functools.partial(_ffn_prompt_kernel, tiles_per_seq=tiles_per_seq),
        grid=(n // tm,),
        in_specs=[pl.BlockSpec((tm, D_MODEL), row), _resident((1, D_MODEL)), _resident((D_MODEL, 2 * D_FF)),
                  _resident((SUBLANES, 2 * D_FF)), _resident((D_FF, D_MODEL)), _resident((1, D_MODEL))],
        out_specs=[pl.BlockSpec((tm, D_MODEL), row), pl.BlockSpec((1, SUBLANES, 2 * D_FF), lambda i: (i // tiles_per_seq, 0, 0))],
        out_shape=[jax.ShapeDtypeStruct((n, D_MODEL), f32), jax.ShapeDtypeStruct((n // seq_len, SUBLANES, 2 * D_FF), f32)],
        scratch_shapes=[pltpu.VMEM((2 * FF_NBLK, SUBLANES, FF_BLOCK), f32), pltpu.VMEM((SUBLANES + tm, FF_BLOCK), f32),
                        pltpu.VMEM((tm, D_FF), bf16)],
        compiler_params=pltpu.CompilerParams(dimension_semantics=("arbitrary",), vmem_limit_bytes=VMEM_LIMIT),
        name="ffn_prompt",
    )(x, gpre, up, cw, down, gpost)


def _ffn_decode(x, gpre, up, cw, down, gpost, buf):
    n = x.shape[0]
    return pl.pallas_call(
        _ffn_decode_kernel,
        grid=(1,),
        in_specs=[_resident((n, D_MODEL)), _resident((1, D_MODEL)), _resident((D_MODEL, 2 * D_FF)),
                  _resident((SUBLANES, 2 * D_FF)), _resident((D_FF, D_MODEL)), _resident((1, D_MODEL)),
                  _resident((FFN_CONV - 1, n, 2 * D_FF))],
        out_specs=[pl.BlockSpec((n, D_MODEL), lambda i: (0, 0)), pl.BlockSpec((FFN_CONV - 1, n, 2 * D_FF), lambda i: (0, 0, 0))],
        out_shape=[jax.ShapeDtypeStruct((n, D_MODEL), f32), jax.ShapeDtypeStruct((FFN_CONV - 1, n, 2 * D_FF), f32)],
        scratch_shapes=[pltpu.VMEM((n, D_FF), bf16)],
        compiler_params=pltpu.CompilerParams(dimension_semantics=("arbitrary",), vmem_limit_bytes=VMEM_LIMIT),
        name="ffn_decode",
    )(x, gpre, up, cw, down, gpost, buf)


PAGES_PER_STEP = 8


def _page_specs(layer, pg, block):
    nd = len(block)

    def spec(k):
        return pl.BlockSpec((1, 1) + block, lambda b, g, pt: (layer, pt[b, g * pg + k]) + (0,) * nd)

    return [spec(k) for k in range(pg)]


def _dec_scores_kernel(pt_ref, iq_ref, iw_ref, *refs):
    pages, o_ref = refs[:-1], refs[-1]
    iq = iq_ref[0]
    iw = iw_ref[0]
    ps = pages[0].shape[2]
    for k, pr in enumerate(pages):
        s = _nt(iq, pr[0, 0].astype(bf16))
        o_ref[0, :, k * ps:(k + 1) * ps] = jnp.sum(iw * jnp.maximum(s, 0.0), axis=0, keepdims=True)


def _dec_scores(layer, page_table, iq8, iw8, pool):
    s, npages = page_table.shape
    ps, dim = pool.shape[2], pool.shape[3]
    pg = min(PAGES_PER_STEP, npages)
    grid_spec = pltpu.PrefetchScalarGridSpec(
        num_scalar_prefetch=1, grid=(s, npages // pg),
        in_specs=[pl.BlockSpec((1, SUBLANES, dim), lambda b, g, pt: (b, 0, 0)),
                  pl.BlockSpec((1, SUBLANES, 1), lambda b, g, pt: (b, 0, 0))] + _page_specs(layer, pg, (ps, dim)),
        out_specs=pl.BlockSpec((1, 1, pg * ps), lambda b, g, pt: (b, 0, g)))
    return pl.pallas_call(
        _dec_scores_kernel, grid_spec=grid_spec,
        out_shape=jax.ShapeDtypeStruct((s, 1, npages * ps), f32),
        compiler_params=pltpu.CompilerParams(dimension_semantics=("arbitrary", "arbitrary"), vmem_limit_bytes=VMEM_LIMIT),
        name="dec_scores",
    )(page_table, iq8, iw8, *([pool] * pg))


def _dec_select_kernel(sc_in_ref, iq_ref, iw_ref, ik_ref, ones_ref, thr_ref, j_ref, snew_ref, sc_ref, *, topk):
    nc, s, _ = sc_in_ref.shape

    def copy(c, carry):
        sc_ref[c] = sc_in_ref[c]
        return carry

    lax.fori_loop(0, nc, copy, 0)
    prod = iq_ref[...].astype(f32) * ik_ref[...].astype(f32)
    dots = jnp.sum(prod, axis=2, keepdims=True)
    snew = jnp.sum(iw_ref[...] * jnp.maximum(dots, 0.0), axis=1)
    snew = jnp.broadcast_to(snew, (s, LANES))
    lane = lax.broadcasted_iota(i32, (s, KV_BLOCK), 1)
    sc_ref[nc] = jnp.where(lane == 0, _tile2(snew), -jnp.inf)
    idx_bits = max(1, int(math.ceil(math.log2((nc + 1) * KV_BLOCK)))) + 1
    thr = _select_topk(sc_ref, nc + 1, jnp.int32(0), topk, 0, ones_ref[...], j_ref, idx_bits)
    thr_ref[...] = thr
    snew_ref[...] = snew


def _dec_select(scores, iq8, iw8, ik_new, topk):
    s, _, p = scores.shape
    nc = p // KV_BLOCK
    sc = scores.reshape(s, nc, KV_BLOCK).transpose(1, 0, 2)
    return pl.pallas_call(
        functools.partial(_dec_select_kernel, topk=topk),
        grid=(1,),
        in_specs=[_resident((nc, s, KV_BLOCK)), _resident(iq8.shape), _resident(iw8.shape), _resident((s, 1, iq8.shape[2])),
                  _resident((KV_BLOCK, LANES))],
        out_specs=[pl.BlockSpec((s, LANES), lambda i: (0, 0))] * 3,
        out_shape=[jax.ShapeDtypeStruct((s, LANES), f32), jax.ShapeDtypeStruct((s, LANES), i32), jax.ShapeDtypeStruct((s, LANES), f32)],
        scratch_shapes=[pltpu.VMEM((nc + 1, s, KV_BLOCK), f32)],
        compiler_params=pltpu.CompilerParams(dimension_semantics=("arbitrary",), vmem_limit_bytes=VMEM_LIMIT),
        name="dec_select",
    )(sc, iq8, iw8, ik_new[:, None, :], jnp.ones((KV_BLOCK, LANES), bf16))


def _dec_attn_kernel(pt_ref, qm_ref, sc_ref, thr_ref, j_ref, snew_ref, knew_ref, vnew_ref, *refs, npast):
    pg = (len(refs) - 4) // 2
    kpages, vpages = refs[:pg], refs[pg:2 * pg]
    o_ref, m_ref, l_ref, acc_ref = refs[2 * pg:]
    g = pl.program_id(1)
    ps = kpages[0].shape[2]

    @pl.when(g == 0)
    def _():
        m_ref[...] = jnp.full_like(m_ref, NEG)
        l_ref[...] = jnp.zeros_like(l_ref)
        acc_ref[...] = jnp.zeros_like(acc_ref)

    qm = qm_ref[0]
    thr, jcut = thr_ref[0], j_ref[0]
    lane = lax.broadcasted_iota(i32, (1, ps), 1)

    def update(z, sel):
        zz = jnp.where(sel, z, NEG)
        m_old = m_ref[...]
        m_new = jnp.maximum(m_old, jnp.max(zz, axis=1, keepdims=True))
        p = jnp.where(sel, jnp.exp(zz - m_new), 0.0)
        alpha = jnp.exp(m_old - m_new)
        l_ref[...] = alpha * l_ref[...] + jnp.sum(p, axis=1, keepdims=True)
        m_ref[...] = m_new
        return alpha, p

    for k in range(pg):
        s = sc_ref[0, :, k * ps:(k + 1) * ps]
        kidx = (g * pg + k) * ps + lane
        sel = (s > thr) | ((s == thr) & (kidx < jcut))
        z = _nt(qm, kpages[k][0, 0].astype(bf16))
        alpha, p = update(z, sel)
        acc_ref[...] = alpha * acc_ref[...] + _dot(p.astype(bf16), vpages[k][0, 0].astype(bf16))

    @pl.when(g == pl.num_programs(1) - 1)
    def _():
        snew = snew_ref[0]
        sel = ((snew > thr) | ((snew == thr) & (jnp.int32(npast) < jcut)))[:, 0:1]
        z = jnp.sum(qm.astype(f32) * knew_ref[0], axis=1, keepdims=True)
        alpha, p = update(z, sel)
        acc = alpha * acc_ref[...] + p * vnew_ref[0]
        o_ref[0] = acc / l_ref[...]


def _dec_attn(layer, page_table, qm, scores, thr, jcut, snew, k_new, v_new, kpool, vpool):
    s, npages = page_table.shape
    ps = kpool.shape[2]
    pg = min(PAGES_PER_STEP, npages)
    row = lambda b, g, pt: (b, 0, 0)
    rows = lambda a: a[:, None, :]
    grid_spec = pltpu.PrefetchScalarGridSpec(
        num_scalar_prefetch=1, grid=(s, npages // pg),
        in_specs=[pl.BlockSpec((1, SUBLANES, LANES), row), pl.BlockSpec((1, 1, pg * ps), lambda b, g, pt: (b, 0, g))]
        + [pl.BlockSpec((1, 1, LANES), row)] * 5
        + _page_specs(layer, pg, kpool.shape[2:]) + _page_specs(layer, pg, vpool.shape[2:]),
        out_specs=pl.BlockSpec((1, SUBLANES, LANES), row),
        scratch_shapes=[pltpu.VMEM((SUBLANES, 1), f32), pltpu.VMEM((SUBLANES, 1), f32), pltpu.VMEM((SUBLANES, LANES), f32)])
    return pl.pallas_call(
        functools.partial(_dec_attn_kernel, npast=npages * ps), grid_spec=grid_spec,
        out_shape=jax.ShapeDtypeStruct((s, SUBLANES, LANES), f32),
        compiler_params=pltpu.CompilerParams(dimension_semantics=("arbitrary", "arbitrary"), vmem_limit_bytes=VMEM_LIMIT),
        name="dec_attn",
    )(page_table, qm, scores, rows(thr), rows(jcut), rows(snew), rows(k_new), rows(v_new), *([kpool] * pg), *([vpool] * pg))


def _dec_sb_kernel(pt_ref, q_ref, uo_ref, *refs):
    pg = (len(refs) - 3) // 2
    kpages, vpages = refs[:pg], refs[pg:2 * pg]
    o_ref, r_ref, acc_ref = refs[2 * pg:]
    g = pl.program_id(1)
    ps = kpages[0].shape[2]

    @pl.when(g == 0)
    def _():
        r_ref[...] = jnp.zeros_like(r_ref)
        acc_ref[...] = jnp.zeros_like(acc_ref)

    q = jnp.broadcast_to(q_ref[0].astype(f32), (SUBLANES, MIX_W))
    rowh = lax.broadcasted_iota(i32, (SUBLANES, MIX_W), 0)
    laneh = lax.broadcasted_iota(i32, (SUBLANES, MIX_W), 1) // HEAD_DIM
    diag = rowh == laneh
    qm = jnp.where(diag, q, 0.0).astype(bf16)
    uo = uo_ref[...]
    for k in reversed(range(pg)):
        z = _nt(qm, kpages[k][0, 0].astype(bf16))
        lk = -_softplus(z)
        hi = lk.astype(bf16)
        lo = (lk - hi.astype(f32)).astype(bf16)
        res = _dot(jnp.concatenate([hi, lo], axis=1), uo)
        r = r_ref[...]
        a = jnp.exp(z + lk + res[:, :LANES] + r)
        r_ref[...] = r + res[:, LANES:]
        acc_ref[...] += _dot(a.astype(bf16), vpages[k][0, 0].astype(bf16))

    @pl.when(g == pl.num_programs(1) - 1)
    def _():
        o_ref[0] = jnp.sum(jnp.where(diag, acc_ref[...], 0.0), axis=0, keepdims=True).astype(o_ref.dtype)


def _dec_stickbreak(layer, page_table, q, kpool, vpool):
    s, npages = page_table.shape
    ps = kpool.shape[2]
    assert ps == LANES
    pg = min(PAGES_PER_STEP, npages)
    ng = npages // pg

    def rev_specs(pool):
        nd = len(pool.shape) - 2
        return [pl.BlockSpec((1, 1) + pool.shape[2:], (lambda b, g, pt, k=k: (layer, pt[b, (ng - 1 - g) * pg + k]) + (0,) * nd))
                for k in range(pg)]

    grid_spec = pltpu.PrefetchScalarGridSpec(
        num_scalar_prefetch=1, grid=(s, ng),
        in_specs=[pl.BlockSpec((1, 1, MIX_W), lambda b, g, pt: (b, 0, 0)),
                  pl.BlockSpec((2 * LANES, 2 * LANES), lambda b, g, pt: (0, 0))] + rev_specs(kpool) + rev_specs(vpool),
        out_specs=pl.BlockSpec((1, 1, MIX_W), lambda b, g, pt: (b, 0, 0)),
        scratch_shapes=[pltpu.VMEM((SUBLANES, LANES), f32), pltpu.VMEM((SUBLANES, MIX_W), f32)])
    out = pl.pallas_call(
        _dec_sb_kernel, grid_spec=grid_spec,
        out_shape=jax.ShapeDtypeStruct((s, 1, MIX_W), bf16),
        compiler_params=pltpu.CompilerParams(dimension_semantics=("arbitrary", "arbitrary"), vmem_limit_bytes=VMEM_LIMIT),
        name="dec_stickbreak",
    )(page_table, q[:, None, :], _suffix_ones(), *([kpool] * pg), *([vpool] * pg))
    return out[:, 0, :]


def _dec_rec_kernel(q_ref, k_ref, vcol_ref, cocol_ref, cnormcol_ref, small_ref, cbias_ref, cst_ref, nst_ref, mst_ref,
                    xrow_ref, convst_ref, convw_ref, xcol_ref, convstcol_ref, convwcol_ref, dzcol_ref, dnormcol_ref,
                    ddcol_ref, dtb_ref, alog_ref, hst_ref,
                    oc_ref, cnew_ref, nnew_ref, mnew_ref, od_ref, convnew_ref, hnew_ref):
    small = small_ref[0]
    q_row, k_row, v_col = q_ref[0], k_ref[0], vcol_ref[0]
    gates = small + cbias_ref[...]
    lane = lax.broadcasted_iota(i32, (1, LANES), 1)
    m_row = jnp.zeros((1, LANES), f32)
    for h in range(N_HEADS):
        hs = slice(h * HEAD_DIM, (h + 1) * HEAD_DIM)
        q_h, k_h, v_h = q_row[:, hs], k_row[:, hs], v_col[hs, :]
        cm = cst_ref[0, 0, h]
        n_h = nst_ref[0, 0, h:h + 1, :]
        m_h = mst_ref[0, 0, :, h:h + 1]
        ig = gates[:, S_CI + h:S_CI + h + 1]
        lf = _log_sigmoid(gates[:, S_CF + h:S_CF + h + 1])
        log_inter = lf + m_h
        mt = jnp.maximum(log_inter, ig)
        sc = jnp.sum(q_h * k_h, axis=1, keepdims=True) * jnp.exp(ig - mt)
        w_inter = jnp.exp(log_inter - mt)
        num = sc * v_h + w_inter * jnp.sum(cm * q_h, axis=1, keepdims=True)
        den = sc + w_inter * jnp.sum(n_h * q_h, axis=1, keepdims=True)
        hh = num / jnp.maximum(jnp.abs(den), jnp.exp(-mt))
        hn = hh * lax.rsqrt(jnp.mean(hh * hh, axis=0, keepdims=True) + EPS)
        oc_ref[0, hs, :] = jax.nn.sigmoid(cocol_ref[0, hs, :]) * hn * cnormcol_ref[hs, :]
        u = jnp.exp(ig - mt)
        dec = jnp.exp(lf + m_h - mt)
        cnew_ref[0, h] = dec * cm + (u * v_h) * k_h
        nnew_ref[0, h:h + 1, :] = dec * n_h + u * k_h
        m_row = jnp.where(lane == h, mt, m_row)
    mnew_ref[0] = m_row[:, :N_HEADS]
    cs = convst_ref[0, 0]
    xn = xrow_ref[0]
    w = convw_ref[...]
    y = w[D_CONV:D_CONV + 1] + w[D_CONV - 1:D_CONV] * xn
    for j in range(D_CONV - 1):
        y = y + w[j:j + 1] * cs[j:j + 1]
    xbc = y * jax.nn.sigmoid(y)
    convnew_ref[0, 0:D_CONV - 2, :] = cs[1:D_CONV - 1]
    convnew_ref[0, D_CONV - 2:D_CONV - 1, :] = xn
    ycol = convwcol_ref[D_CONV] + convwcol_ref[D_CONV - 1] * xcol_ref[0]
    for j in range(D_CONV - 1):
        ycol = ycol + convwcol_ref[j] * convstcol_ref[0, j]
    xcol = ycol * jax.nn.sigmoid(ycol)
    dt_row = _softplus(small + dtb_ref[...])
    a_row = dt_row * (-jnp.exp(alog_ref[...]))
    yds = []
    for h in range(N_HEADS):
        g = h // (N_HEADS // D_GROUPS)
        hs = slice(h * HEAD_DIM, (h + 1) * HEAD_DIM)
        b_g = xbc[:, MIX_W + g * D_STATE:MIX_W + (g + 1) * D_STATE]
        c_g = xbc[:, MIX_W + D_GROUPS * D_STATE + g * D_STATE:MIX_W + D_GROUPS * D_STATE + (g + 1) * D_STATE]
        dt_h = dt_row[:, S_DT + h:S_DT + h + 1]
        ea = jnp.exp(a_row[:, S_DT + h:S_DT + h + 1])
        x_h = xcol[hs, :]
        hm = hst_ref[0, 0, h]
        cb = jnp.sum(c_g * b_g, axis=1, keepdims=True)
        yh = cb * dt_h * x_h + ea * jnp.sum(hm * c_g, axis=1, keepdims=True)
        yds.append(yh + ddcol_ref[hs, :] * x_h)
        hnew_ref[0, h] = ea * hm + (dt_h * x_h) * b_g
    dz = dzcol_ref[0]
    t = jnp.concatenate(yds, axis=0) * (dz * jax.nn.sigmoid(dz))
    od_ref[0] = t * lax.rsqrt(jnp.mean(t * t, axis=0, keepdims=True) + EPS) * dnormcol_ref[...]


def _dec_recurrent(layer, q, k, v, co, cnorm, small, cbias_row, c_st, n_st, m_st,
                   dxbc, conv_st, convw, dz, dnorm, dd, dtb_row, alog_row, h_st):
    s = q.shape[0]
    col = lambda a: a.astype(f32)[..., None]
    row3 = lambda a: a.astype(f32)[:, None, :]
    per = lambda *blk: pl.BlockSpec((1,) + blk, lambda b: (b,) + (0,) * len(blk))
    st = lambda *blk: pl.BlockSpec((1, 1) + blk, lambda b: (layer, b) + (0,) * len(blk))
    convw_col = col(convw[:D_CONV + 1, :MIX_W])
    ins = [
        (row3(q), per(1, MIX_W)), (row3(k), per(1, MIX_W)), (col(v), per(MIX_W, 1)), (col(co), per(MIX_W, 1)),
        (col(cnorm[0]), _resident((MIX_W, 1))), (row3(small), per(1, LANES)), (cbias_row, _resident((1, LANES))),
        (c_st, st(N_HEADS, HEAD_DIM, HEAD_DIM)), (n_st, st(N_HEADS, HEAD_DIM)), (m_st[:, :, None, :], st(1, N_HEADS)),
        (row3(dxbc), per(1, D_CONV_CH)), (conv_st, st(D_CONV - 1, D_CONV_CH)), (convw, _resident((SUBLANES, D_CONV_CH))),
        (col(dxbc[:, :MIX_W]), per(MIX_W, 1)), (col(conv_st[layer][:, :, :MIX_W]), per(D_CONV - 1, MIX_W, 1)),
        (convw_col, _resident((D_CONV + 1, MIX_W, 1))), (col(dz), per(MIX_W, 1)), (col(dnorm[0]), _resident((MIX_W, 1))),
        (col(dd[0]), _resident((MIX_W, 1))), (dtb_row, _resident((1, LANES))), (alog_row, _resident((1, LANES))),
        (h_st, st(N_HEADS, HEAD_DIM, D_STATE)),
    ]
    outs = [
        (jax.ShapeDtypeStruct((s, MIX_W, 1), f32), per(MIX_W, 1)),
        (jax.ShapeDtypeStruct((s, N_HEADS, HEAD_DIM, HEAD_DIM), f32), per(N_HEADS, HEAD_DIM, HEAD_DIM)),
        (jax.ShapeDtypeStruct((s, N_HEADS, HEAD_DIM), f32), per(N_HEADS, HEAD_DIM)),
        (jax.ShapeDtypeStruct((s, 1, N_HEADS), f32), per(1, N_HEADS)),
        (jax.ShapeDtypeStruct((s, MIX_W, 1), f32), per(MIX_W, 1)),
        (jax.ShapeDtypeStruct((s, D_CONV - 1, D_CONV_CH), f32), per(D_CONV - 1, D_CONV_CH)),
        (jax.ShapeDtypeStruct((s, N_HEADS, HEAD_DIM, D_STATE), f32), per(N_HEADS, HEAD_DIM, D_STATE)),
    ]
    oc, c1, n1, m1, od, conv1, h1 = pl.pallas_call(
        _dec_rec_kernel,
        grid=(s,),
        in_specs=[sp for _, sp in ins],
        out_specs=[sp for _, sp in outs],
        out_shape=[sh for sh, _ in outs],
        compiler_params=pltpu.CompilerParams(dimension_semantics=("arbitrary",), vmem_limit_bytes=VMEM_LIMIT),
        name="dec_recurrent",
    )(*[a for a, _ in ins])
    return oc[:, :, 0], c1, n1, m1[:, 0, :], od[:, :, 0], conv1, h1


_IN_SPLITS = (256, 128, 128, 256, 64, 4, 256, 256, 256, 256, 256, 256, 4, 4, 256, 256, 768, 4, 4096)
_IN_OFFS = np.concatenate([[0], np.cumsum(_IN_SPLITS)]).tolist()
(_A_Q, _A_K, _A_V, _A_IQ, _A_IK, _A_IW, _B_Q, _B_K, _B_V, _C_Q, _C_K, _C_V, _C_I, _C_F, _C_O,
 _D_Z, _D_XBC, _D_DT, _GATE) = range(len(_IN_SPLITS))


def _lane_row(width, items):
    r = jnp.zeros((1, width), f32)
    for off, v in items:
        r = r.at[0, off:off + v.shape[0]].set(v.astype(f32))
    return r


def _prep_layer(l, p):
    w_in = p["w_in"][l]
    seg = lambda i: w_in[:, _IN_OFFS[i]:_IN_OFFS[i + 1]]
    rep2 = lambda w: jnp.concatenate([w[:, :HEAD_DIM], w[:, :HEAD_DIM], w[:, HEAD_DIM:], w[:, HEAD_DIM:]], axis=1)
    small = jnp.concatenate([seg(_A_IW) * (N_HEADS ** -0.5 * HEAD_DIM ** -0.5), seg(_C_I), seg(_C_F), seg(_D_DT),
                             jnp.zeros((D_MODEL, LANES - 4 * N_HEADS), f32)], axis=1)
    scale = HEAD_DIM ** -0.5
    w_proj = jnp.concatenate([
        seg(_A_Q) * scale, rep2(seg(_A_K)), rep2(seg(_A_V)), seg(_A_IQ), jnp.tile(seg(_A_IK), (1, N_HEADS)),
        seg(_B_Q) * scale, seg(_B_K), seg(_B_V),
        seg(_C_Q), seg(_C_K) * scale, seg(_C_V), seg(_C_O),
        seg(_D_Z), seg(_D_XBC), small], axis=1).astype(bf16)
    row = lambda a: a[None, :].astype(f32)
    pad_rows = lambda a: jnp.concatenate([a, jnp.zeros((SUBLANES - a.shape[0], a.shape[1]), f32)], axis=0)
    return dict(
        w_proj=w_proj, w_gate=seg(_GATE).astype(bf16),
        g_mix_pre=row(p["norm_mix_pre"][l]), g_mix_post=row(p["norm_mix_post"][l]),
        g_ffn_pre=row(p["norm_ffn_pre"][l]), g_ffn_post=row(p["norm_ffn_post"][l]),
        cbias_row=_lane_row(LANES, [(S_CI, p["c_b_i"][l]), (S_CF, p["c_b_f"][l])]),
        cnorm=row(p["c_norm"][l]),
        convw=pad_rows(jnp.concatenate([p["d_conv_w"][l], p["d_conv_b"][l][None]], axis=0)),
        dtb_row=_lane_row(LANES, [(S_DT, p["d_dt_bias"][l])]), alog_row=_lane_row(LANES, [(S_DT, p["d_A_log"][l])]),
        dd_row=row(jnp.repeat(p["d_D"][l], HEAD_DIM)), dnorm=row(p["d_norm"][l]),
        wbr=jnp.stack([p["w_br_a"][l], p["w_br_b"][l], p["w_br_c"][l], p["w_br_d"][l]]).astype(bf16),
        wout=p["w_out"][l].astype(bf16),
        up=p["ffn_up"][l].astype(bf16), down=p["ffn_down"][l].astype(bf16),
        ffn_cw=pad_rows(jnp.concatenate([p["ffn_conv_w"][l], p["ffn_conv_b"][l][None]], axis=0)),
    )


def _rope_table(pos):
    half = HEAD_DIM // 2
    inv = jnp.power(ROPE_THETA, -jnp.arange(half, dtype=f32) / half)
    ang = pos.astype(f32)[:, None] * inv[None, :]
    cos, sin = jnp.cos(ang), jnp.sin(ang)
    return jnp.concatenate([jnp.tile(cos, (1, 4)), jnp.tile(jnp.concatenate([-sin, sin], axis=1), (1, 2))], axis=1)


def _row_tile(n):
    for tm in (512, 256, 128, 64, 32, 16, 8):
        if n % tm == 0:
            return tm
    raise ValueError(f"row count {n} is not a multiple of 8")


def _prompt_layer(x, cs, w, bsz, t):
    n = x.shape[0]
    tm = min(_row_tile(t), _row_tile(n))
    (qa, ka, karep, va, varep, iq, ik, ik4, qb, kb, kbh, vb, vbh, qc, kc, vc, co, dz, dxbc, small) = _proj(
        x, w["g_mix_pre"], w["w_proj"], cs, tm)
    r3 = lambda a: a.reshape(bsz, t, a.shape[-1])
    small3 = r3(small)
    o_a = _dsa_prompt(r3(iq), small3, r3(qa), r3(ik4), r3(karep), r3(varep))
    o_b = _stickbreak_prompt(r3(qb), r3(kbh), r3(vbh))
    o_c, s_ext, m_ext = _mlstm_prompt(r3(qc), r3(kc), r3(vc), r3(co), small3, w["cbias_row"], w["cnorm"],
                                      jnp.zeros((bsz, MIX_W, C_EXT), f32), jnp.zeros((bsz, SUBLANES, LANES), f32))
    dxbc3 = r3(dxbc)
    o_d, ht = _ssd_prompt(dxbc3, jnp.zeros((bsz, SUBLANES, D_CONV_CH), f32), r3(dz), small3, w["convw"], w["dtb_row"],
                          w["alog_row"], w["dd_row"], w["dnorm"], jnp.zeros((bsz, D_GROUPS, D_STATE, LANES), f32))
    r2 = lambda a: a.reshape(n, MIX_W)
    x1 = _merge(x, w["g_mix_pre"], w["w_gate"], r2(o_a), r2(o_b), r2(o_c), r2(o_d), w["wbr"], w["wout"], w["g_mix_post"], tm)
    x2, tail = _ffn_prompt(x1, w["g_ffn_pre"], w["up"], w["ffn_cw"], w["down"], w["g_ffn_post"], t, tm)
    c1, n1, m1 = _mlstm_state_from_ext(s_ext, m_ext)
    state = (ka.reshape(bsz, t, 2, HEAD_DIM), va.reshape(bsz, t, 2, HEAD_DIM), ik.reshape(bsz, t, HEAD_DIM),
             kb.reshape(bsz, t, N_HEADS, HEAD_DIM), vb.reshape(bsz, t, N_HEADS, HEAD_DIM), c1, n1, m1,
             dxbc3[:, t - (D_CONV - 1):, :], _ssd_state_from_ext(ht), tail[:, SUBLANES - (FFN_CONV - 1):, :])
    return x2, state


def _sample_layer(l, x, cs, w, st):
    s = x.shape[0]
    pt = st["page_table"]
    (qa, ka, karep, va, varep, iq, ik, ik4, qb, kb, kbh, vb, vbh, qc, kc, vc, co, dz, dxbc, small) = _proj(
        x, w["g_mix_pre"], w["w_proj"], cs, s)
    pad8 = lambda a: jnp.concatenate([a, jnp.zeros((s, SUBLANES - N_HEADS) + a.shape[2:], a.dtype)], axis=1)
    iq8 = pad8(iq.reshape(s, N_HEADS, HEAD_DIM))
    iw8 = pad8(small[:, S_IW:S_IW + N_HEADS, None])
    a_kidx, a_k, a_v, b_k, b_v = st["a_kidx"], st["a_k"], st["a_v"], st["b_k"], st["b_v"]
    depth, n_pool, page = a_k.shape[:3]
    npast = pt.shape[1] * page
    scores = _dec_scores(l, pt, iq8, iw8, a_kidx)
    thr, jcut, snew = _dec_select(scores, iq8, iw8, ik, min(TOPK_MAX, (npast + 1) // 4))
    q4 = qa.reshape(s, N_HEADS, HEAD_DIM)
    zero = jnp.zeros_like(q4[:, 0])
    qm = pad8(jnp.stack([jnp.concatenate([q4[:, 0], zero], -1), jnp.concatenate([q4[:, 1], zero], -1),
                         jnp.concatenate([zero, q4[:, 2]], -1), jnp.concatenate([zero, q4[:, 3]], -1)], axis=1))
    flat = lambda pool: pool.reshape(depth, n_pool, page, -1)
    o8 = _dec_attn(l, pt, qm, scores, thr, jcut, snew, ka, va, flat(a_k), flat(a_v))
    o_a = jnp.concatenate([o8[:, 0, :HEAD_DIM], o8[:, 1, :HEAD_DIM], o8[:, 2, HEAD_DIM:], o8[:, 3, HEAD_DIM:]], axis=-1).astype(bf16)
    o_b = _dec_stickbreak(l, pt, qb, flat(b_k), flat(b_v))
    o_c, c1, n1, m1, o_d, conv1, h1 = _dec_recurrent(
        l, qc, kc, vc, co, w["cnorm"], small, w["cbias_row"], st["c_C"], st["c_n"], st["c_m"],
        dxbc, st["d_conv"], w["convw"], dz, w["dnorm"], w["dd_row"], w["dtb_row"], w["alog_row"], st["d_ssm"])
    x1 = _merge(x, w["g_mix_pre"], w["w_gate"], o_a, o_b, o_c.astype(bf16), o_d.astype(bf16), w["wbr"], w["wout"],
                w["g_mix_post"], s)
    x2, buf = _ffn_decode(x1, w["g_ffn_pre"], w["up"], w["ffn_cw"], w["down"], w["g_ffn_post"],
                          jnp.swapaxes(st["ffn_conv"][l], 0, 1))
    state = (ka.reshape(s, 1, 2, HEAD_DIM), va.reshape(s, 1, 2, HEAD_DIM), ik.reshape(s, 1, HEAD_DIM),
             kb.reshape(s, 1, N_HEADS, HEAD_DIM), vb.reshape(s, 1, N_HEADS, HEAD_DIM), c1, n1, m1, conv1, h1,
             jnp.swapaxes(buf, 0, 1))
    return x2, state


def kernel(x_prompt, x_sample, cache_a_k, cache_a_v, cache_a_kidx, cache_b_k, cache_b_v, state_c_C, state_c_n, state_c_m,
           state_d_conv, state_d_ssm, state_ffn_conv, page_table, norm_mix_pre, norm_mix_post, w_in, c_b_i, c_b_f, c_norm,
           d_conv_w, d_conv_b, d_dt_bias, d_A_log, d_D, d_norm, w_br_a, w_br_b, w_br_c, w_br_d, w_out, norm_ffn_pre,
           norm_ffn_post, ffn_up, ffn_conv_w, ffn_conv_b, ffn_down):
    params = dict(norm_mix_pre=norm_mix_pre, norm_mix_post=norm_mix_post, w_in=w_in, c_b_i=c_b_i, c_b_f=c_b_f, c_norm=c_norm,
                  d_conv_w=d_conv_w, d_conv_b=d_conv_b, d_dt_bias=d_dt_bias, d_A_log=d_A_log, d_D=d_D, d_norm=d_norm,
                  w_br_a=w_br_a, w_br_b=w_br_b, w_br_c=w_br_c, w_br_d=w_br_d, w_out=w_out, norm_ffn_pre=norm_ffn_pre,
                  norm_ffn_post=norm_ffn_post, ffn_up=ffn_up, ffn_conv_w=ffn_conv_w, ffn_conv_b=ffn_conv_b, ffn_down=ffn_down)
    st = dict(page_table=page_table, a_k=cache_a_k, a_v=cache_a_v, a_kidx=cache_a_kidx, b_k=cache_b_k, b_v=cache_b_v,
              c_C=state_c_C, c_n=state_c_n, c_m=state_c_m, d_conv=state_d_conv, d_ssm=state_d_ssm, ffn_conv=state_ffn_conv)
    bsz, t, _ = x_prompt.shape
    s, ts, _ = x_sample.shape
    assert ts == 1, "the sample group is decoded one token at a time"
    depth = w_in.shape[0]
    past = page_table.shape[1] * cache_a_k.shape[2]
    cs_p = _rope_table(jnp.arange(t, dtype=i32))
    cs_s = _rope_table(jnp.full((s,), past, dtype=i32))
    xp = x_prompt.reshape(bsz * t, D_MODEL)
    xs = x_sample.reshape(s, D_MODEL)
    p_list, s_list = [], []
    for l in range(depth):
        w = _prep_layer(l, params)
        xp, sp = _prompt_layer(xp, cs_p, w, bsz, t)
        xs, ss = _sample_layer(l, xs, cs_s, w, st)
        p_list.append(sp)
        s_list.append(ss)
    p_st = [jnp.stack([sl[i] for sl in p_list]) for i in range(11)]
    s_st = [jnp.stack([sl[i] for sl in s_list]) for i in range(11)]
    return (xp.reshape(bsz, t, D_MODEL), xs.reshape(s, 1, D_MODEL), *p_st, *s_st)
```

```python
import functools
import math

import jax
import jax.numpy as jnp
import numpy as np
from jax import lax
from jax.experimental import pallas as pl
from jax.experimental.pallas import tpu as pltpu

f32 = jnp.float32
bf16 = jnp.bfloat16
i32 = jnp.int32

D_MODEL = 1024
HEAD_DIM = 64
N_HEADS = 4
MIX_W = N_HEADS * HEAD_DIM
D_STATE = 128
D_GROUPS = 2
D_CONV = 4
D_CONV_CH = MIX_W + 2 * D_GROUPS * D_STATE
D_FF = 2816
FFN_CONV = 3
TOPK_MAX = 256
ROPE_THETA = 10000.0
EPS = 1e-6
NEG = -1e30
LANES = 128
SUBLANES = 8
CHUNK = 128
KV_BLOCK = 256
VMEM_LIMIT = 56 * 1024 * 1024

P_QA, P_KA, P_VA, P_IQ, P_IK = 0, 256, 512, 768, 1024
P_QB, P_KB, P_VB = 1280, 1536, 1792
P_QC, P_KC, P_VC, P_CO = 2048, 2304, 2560, 2816
P_DZ, P_DXBC, P_SMALL = 3072, 3328, 4096
P_W = 4224
S_IW, S_CI, S_CF, S_DT = 0, 4, 8, 12


def _nt(a, b):
    return lax.dot_general(a, b, (((1,), (1,)), ((), ())), preferred_element_type=f32)


def _tn(a, b):
    return lax.dot_general(a, b, (((0,), (0,)), ((), ())), preferred_element_type=f32)


def _dot(a, b):
    return jnp.dot(a, b, preferred_element_type=f32)


def _split3(x):
    h = x.astype(bf16)
    r = x - h.astype(f32)
    m = r.astype(bf16)
    l = (r - m.astype(f32)).astype(bf16)
    return h, m, l


def _dot_exact_lhs(t01, x):
    h, m, l = _split3(x)
    return _dot(t01, h) + _dot(t01, m) + _dot(t01, l)


def _head_lane_mask(shape, h, width=HEAD_DIM):
    lane = lax.broadcasted_iota(i32, shape, len(shape) - 1)
    return (lane >= h * width) & (lane < (h + 1) * width)


def _expand_heads(cols, rows):
    out = jnp.broadcast_to(cols[3], (rows, MIX_W))
    for h in (2, 1, 0):
        out = jnp.where(_head_lane_mask((rows, MIX_W), h), jnp.broadcast_to(cols[h], (rows, MIX_W)), out)
    return out


def _rms(x, g):
    return x * lax.rsqrt(jnp.mean(x * x, axis=-1, keepdims=True) + EPS) * g


def _rope_seg(v, cs):
    c, s = cs[:, :LANES], cs[:, LANES:]
    lane = lax.broadcasted_iota(i32, (v.shape[0], LANES), 1)
    first = (lane & 32) == 0
    outs = []
    for j in range(2):
        u = v[:, j * LANES:(j + 1) * LANES]
        sw = jnp.where(first, pltpu.roll(u, LANES - 32, 1), pltpu.roll(u, 32, 1))
        outs.append(u * c + sw * s)
    return jnp.concatenate(outs, axis=1)


def _proj_kernel(x_ref, g_ref, w_ref, cs_ref,
                 qa_ref, ka_ref, karep_ref, va_ref, varep_ref, iq_ref, ik_ref, ik4_ref,
                 qb_ref, kb_ref, kbh_ref, vb_ref, vbh_ref,
                 qc_ref, kc_ref, vc_ref, co_ref, dz_ref, dxbc_ref, small_ref):
    xn = _rms(x_ref[...], g_ref[...]).astype(bf16)
    cs = cs_ref[...]
    tm = xn.shape[0]

    def seg(off, width=MIX_W):
        return _dot(xn, w_ref[:, off:off + width])

    lane = lax.broadcasted_iota(i32, (tm, LANES), 1)
    qa_ref[...] = _rope_seg(seg(P_QA), cs).astype(bf16)
    ka = _rope_seg(seg(P_KA), cs)
    karep_ref[...] = ka.astype(bf16)
    ka_ref[...] = jnp.where(lane < HEAD_DIM, ka[:, :LANES], ka[:, LANES:])
    va = seg(P_VA)
    varep_ref[...] = va.astype(bf16)
    va_ref[...] = jnp.where(lane < HEAD_DIM, va[:, :LANES], va[:, LANES:])
    iq_ref[...] = _rope_seg(seg(P_IQ), cs).astype(bf16)
    ik4 = _rope_seg(seg(P_IK), cs)
    ik4_ref[...] = ik4.astype(bf16)
    ik_ref[...] = ik4[:, :HEAD_DIM]
    qb_ref[...] = seg(P_QB).astype(bf16)
    kb = seg(P_KB)
    kb_ref[...] = kb
    kbh_ref[...] = kb.astype(bf16)
    vb = seg(P_VB)
    vb_ref[...] = vb
    vbh_ref[...] = vb.astype(bf16)
    qc_ref[...] = seg(P_QC).astype(bf16)
    kc_ref[...] = seg(P_KC).astype(bf16)
    vc_ref[...] = seg(P_VC).astype(bf16)
    co_ref[...] = seg(P_CO)
    dz_ref[...] = seg(P_DZ)
    for j in range(3):
        dxbc_ref[:, j * MIX_W:(j + 1) * MIX_W] = seg(P_DXBC + j * MIX_W)
    small_ref[...] = seg(P_SMALL, LANES)


def _proj(x, g, w, cs, tm):
    n = x.shape[0]
    nt = n // tm
    period = cs.shape[0] // tm
    row = lambda i: (i, 0)
    const = lambda i: (0, 0)

    def o(width, dt):
        return jax.ShapeDtypeStruct((n, width), dt), pl.BlockSpec((tm, width), row)

    outs = [o(256, bf16), o(128, f32), o(256, bf16), o(128, f32), o(256, bf16), o(256, bf16), o(64, f32), o(256, bf16),
            o(256, bf16), o(256, f32), o(256, bf16), o(256, f32), o(256, bf16),
            o(256, bf16), o(256, bf16), o(256, bf16), o(256, f32), o(256, f32), o(768, f32), o(128, f32)]
    return pl.pallas_call(
        _proj_kernel,
        grid=(nt,),
        in_specs=[pl.BlockSpec((tm, D_MODEL), row), _resident((1, D_MODEL)),
                  _resident((D_MODEL, P_W)), pl.BlockSpec((tm, 2 * LANES), lambda i: (i % period, 0))],
        out_specs=[s for _, s in outs],
        out_shape=[s for s, _ in outs],
        compiler_params=pltpu.CompilerParams(dimension_semantics=("arbitrary",), vmem_limit_bytes=VMEM_LIMIT),
        name="proj",
    )(x, g, w, cs)


def _stack_heads(q):
    return jnp.concatenate([jnp.where(_head_lane_mask(q.shape, h), q, jnp.zeros_like(q)) for h in range(N_HEADS)], axis=0)


def _sb_chunk(qm, kc, vc, uo, tq, mask, r_ref, oacc_ref):
    z = _nt(qm, kc)
    lk = -(jnp.maximum(z, 0.0) + jnp.log(1.0 + jnp.exp(-jnp.abs(z))))
    if mask is not None:
        lk = jnp.where(mask, lk, 0.0)
    halves = [None, None]
    for half in (1, 0):
        sl = slice(half * LANES, (half + 1) * LANES)
        lkh = lk[:, sl]
        hi = lkh.astype(bf16)
        lo = (lkh - hi.astype(f32)).astype(bf16)
        res = _dot(jnp.concatenate([hi, lo], axis=1), uo)
        r = r_ref[...]
        a = jnp.exp(z[:, sl] + lkh + res[:, :LANES] + r)
        if mask is not None:
            a = jnp.where(mask[:, sl], a, 0.0)
        r_ref[...] = r + res[:, LANES:]
        halves[half] = a.astype(bf16)
    a = jnp.concatenate(halves, axis=1)
    acc = oacc_ref[...]
    for h in range(N_HEADS):
        vh = jnp.where(_head_lane_mask(vc.shape, h), vc, jnp.zeros_like(vc))
        acc = acc + _dot(a[h * tq:(h + 1) * tq], vh)
    oacc_ref[...] = acc


def _sb_kernel(q_ref, k_ref, v_ref, uo_ref, o_ref, oacc_ref, r_ref):
    tq = q_ref.shape[1]
    i = pl.program_id(1)
    qm = _stack_heads(q_ref[0])
    uo = uo_ref[...]
    oacc_ref[...] = jnp.zeros_like(oacc_ref)
    r_ref[...] = jnp.zeros_like(r_ref)
    rows = lax.broadcasted_iota(i32, (N_HEADS * tq, KV_BLOCK), 0) & (tq - 1)
    cols = lax.broadcasted_iota(i32, (N_HEADS * tq, KV_BLOCK), 1)

    def blk(c):
        start = pl.multiple_of(c * KV_BLOCK, KV_BLOCK)
        return k_ref[0, pl.ds(start, KV_BLOCK), :], v_ref[0, pl.ds(start, KV_BLOCK), :]

    kc, vc = blk(i)
    _sb_chunk(qm, kc, vc, uo, tq, cols < rows, r_ref, oacc_ref)

    def body(j, carry):
        kc, vc = blk(i - 1 - j)
        _sb_chunk(qm, kc, vc, uo, tq, None, r_ref, oacc_ref)
        return carry

    lax.fori_loop(0, i, body, 0)
    o_ref[0] = oacc_ref[...].astype(o_ref.dtype)


def _suffix_ones():
    j = np.arange(2 * LANES)[:, None] % LANES
    s = np.arange(2 * LANES)[None, :]
    return jnp.asarray(np.where(s < LANES, (j > s), True).astype(np.float32), dtype=bf16)


def _stickbreak_prompt(q, k, v):
    b, t, _ = q.shape
    tq = KV_BLOCK
    return pl.pallas_call(
        _sb_kernel,
        grid=(b, t // tq),
        in_specs=[pl.BlockSpec((1, tq, MIX_W), lambda bb, i: (bb, i, 0)),
                  pl.BlockSpec((1, t, MIX_W), lambda bb, i: (bb, 0, 0)),
                  pl.BlockSpec((1, t, MIX_W), lambda bb, i: (bb, 0, 0)),
                  pl.BlockSpec((2 * LANES, 2 * LANES), lambda bb, i: (0, 0))],
        out_specs=pl.BlockSpec((1, tq, MIX_W), lambda bb, i: (bb, i, 0)),
        out_shape=jax.ShapeDtypeStruct((b, t, MIX_W), bf16),
        scratch_shapes=[pltpu.VMEM((tq, MIX_W), f32), pltpu.VMEM((N_HEADS * tq, LANES), f32)],
        compiler_params=pltpu.CompilerParams(dimension_semantics=("arbitrary", "arbitrary"), vmem_limit_bytes=VMEM_LIMIT),
        name="stickbreak",
    )(q, k, v, _suffix_ones())


INT_MIN = -2147483648
IDX_BIG = 1 << 30


def _tile2(v):
    return jnp.concatenate([v, v], axis=1)


def _tile_heads(v):
    return jnp.concatenate([v] * N_HEADS, axis=0)


def _key_to_float(key):
    bits = jnp.where(key >= 0, key, (-key) | jnp.int32(INT_MIN))
    return lax.bitcast_convert_type(bits, f32)


def _row_total(acc):
    return jnp.broadcast_to(jnp.sum(acc, axis=1, keepdims=True), acc.shape)


def _select_topk(sc_ref, nc, n_skip, topk, j_ref, idx_bits):
    tq = sc_ref.shape[1]
    n_skip_f = n_skip.astype(f32)
    kf = jnp.float32(topk)

    rg = min(tq, LANES)

    def count(pred, *row_args):
        parts = []
        for r0 in range(0, tq, rg):
            args = [a[r0:r0 + rg] for a in row_args]

            def body(c, acc, r0=r0, args=args):
                m = jnp.where(pred(sc_ref[c, r0:r0 + rg, :], c, *args), 1.0, 0.0)
                return acc + (m[:, :LANES] + m[:, LANES:])

            parts.append(lax.fori_loop(0, nc, body, jnp.zeros((rg, LANES), f32)))
        return _row_total(jnp.concatenate(parts, axis=0))

    def bit_step(it, prefix):
        cand = prefix + jnp.left_shift(jnp.int32(1), 31 - it)
        candf = _key_to_float(cand)
        n_ge = count(lambda s, c, x: s >= _tile2(x), candf) + jnp.where(jnp.float32(NEG) >= candf, n_skip_f, 0.0)
        return jnp.where(n_ge >= kf, cand, prefix)

    prefix = lax.fori_loop(0, 32, bit_step, jnp.full((tq, LANES), INT_MIN, i32))
    thr = _key_to_float(prefix)
    n_gt = count(lambda s, c, x: s > _tile2(x), thr) + jnp.where(jnp.float32(NEG) > thr, n_skip_f, 0.0)
    n_eq = count(lambda s, c, x: s == _tile2(x), thr)
    room = kf - n_gt
    j_ref[...] = jnp.full((tq, LANES), IDX_BIG, i32)
    need = jnp.max(jnp.where(n_eq > room, 1.0, 0.0)) > 0.0

    @pl.when(need)
    def _():
        lane = lax.broadcasted_iota(i32, (rg, KV_BLOCK), 1)

        def jbit(it, j):
            cand = j | jnp.left_shift(jnp.int32(1), idx_bits - 1 - it)
            n_lt = count(lambda s, c, x, y: (s == _tile2(x)) & (c * KV_BLOCK + lane < _tile2(y)), thr, cand)
            return jnp.where(n_lt <= room, cand, j)

        j_ref[...] = lax.fori_loop(0, idx_bits, jbit, jnp.zeros((tq, LANES), i32))

    return thr


def _dsa_kernel(iq_ref, small_ref, qa_ref, ik_ref, k_ref, v_ref, o_ref, sc_ref, j_ref, m_ref, l_ref, acc_ref, *, topk):
    tq = iq_ref.shape[1]
    t = ik_ref.shape[1]
    i = pl.program_id(1)
    nc = i + 1
    n_skip = t - nc * KV_BLOCK
    lane = lax.broadcasted_iota(i32, (tq, KV_BLOCK), 1)
    qpos = i * tq + lax.broadcasted_iota(i32, (tq, KV_BLOCK), 0)
    iqm = _stack_heads(iq_ref[0])
    iw = small_ref[0][:, S_IW:S_IW + N_HEADS]

    def blk(ref, c):
        return ref[0, pl.ds(pl.multiple_of(c * KV_BLOCK, KV_BLOCK), KV_BLOCK), :]

    def score_blk(c, carry):
        s = _nt(iqm, blk(ik_ref, c))
        sc = iw[:, 0:1] * jnp.maximum(s[:tq], 0.0)
        for h in range(1, N_HEADS):
            sc = sc + iw[:, h:h + 1] * jnp.maximum(s[h * tq:(h + 1) * tq], 0.0)
        sc_ref[c] = jnp.where(c * KV_BLOCK + lane <= qpos, sc, NEG)
        return carry

    lax.fori_loop(0, nc, score_blk, 0)
    idx_bits = max(1, int(math.ceil(math.log2(t)))) + 1
    thr = _select_topk(sc_ref, nc, n_skip, topk, j_ref, idx_bits)
    thr2, j2 = _tile2(thr), _tile2(j_ref[...])
    qm = _stack_heads(qa_ref[0])
    m_ref[...] = jnp.full_like(m_ref, NEG)

    def max_blk(c, carry):
        s = sc_ref[c]
        kidx = c * KV_BLOCK + lane
        sel = jnp.where(((s > thr2) | ((s == thr2) & (kidx < j2))) & (kidx <= qpos), 1.0, 0.0)
        sc_ref[c] = sel
        z = jnp.where(_tile_heads(sel) > 0.5, _nt(qm, blk(k_ref, c)), NEG)
        m_ref[...] = jnp.maximum(m_ref[...], jnp.maximum(z[:, :LANES], z[:, LANES:]))
        return carry

    lax.fori_loop(0, nc, max_blk, 0)
    m_ref[...] = jnp.broadcast_to(jnp.max(m_ref[...], axis=1, keepdims=True), m_ref.shape)
    l_ref[...] = jnp.zeros_like(l_ref)
    acc_ref[...] = jnp.zeros_like(acc_ref)

    def attn_blk(c, carry):
        sel4 = _tile_heads(sc_ref[c]) > 0.5
        p = jnp.where(sel4, jnp.exp(_nt(qm, blk(k_ref, c)) - _tile2(m_ref[...])), 0.0)
        l_ref[...] += p[:, :LANES] + p[:, LANES:]
        pb = p.astype(bf16)
        vc = blk(v_ref, c)
        acc = acc_ref[...]
        for h in range(N_HEADS):
            vh = jnp.where(_head_lane_mask(vc.shape, h), vc, jnp.zeros_like(vc))
            acc = acc + _dot(pb[h * tq:(h + 1) * tq], vh)
        acc_ref[...] = acc
        return carry

    lax.fori_loop(0, nc, attn_blk, 0)
    inv = 1.0 / jnp.sum(l_ref[...], axis=1, keepdims=True)
    o_ref[0] = (acc_ref[...] * _expand_heads([inv[h * tq:(h + 1) * tq] for h in range(N_HEADS)], tq)).astype(o_ref.dtype)


def _dsa_prompt(iq, small, qa, ik4, karep, varep):
    b, t, _ = iq.shape
    tq = KV_BLOCK
    topk = min(TOPK_MAX, t // 4)
    qblk = lambda bb, i: (bb, i, 0)
    whole = lambda bb, i: (bb, 0, 0)
    return pl.pallas_call(
        functools.partial(_dsa_kernel, topk=topk),
        grid=(b, t // tq),
        in_specs=[pl.BlockSpec((1, tq, MIX_W), qblk), pl.BlockSpec((1, tq, LANES), qblk), pl.BlockSpec((1, tq, MIX_W), qblk),
                  pl.BlockSpec((1, t, MIX_W), whole), pl.BlockSpec((1, t, MIX_W), whole), pl.BlockSpec((1, t, MIX_W), whole)],
        out_specs=pl.BlockSpec((1, tq, MIX_W), qblk),
        out_shape=jax.ShapeDtypeStruct((b, t, MIX_W), bf16),
        scratch_shapes=[pltpu.VMEM((t // KV_BLOCK, tq, KV_BLOCK), f32), pltpu.VMEM((tq, LANES), i32),
                        pltpu.VMEM((N_HEADS * tq, LANES), f32), pltpu.VMEM((N_HEADS * tq, LANES), f32),
                        pltpu.VMEM((tq, MIX_W), f32)],
        compiler_params=pltpu.CompilerParams(dimension_semantics=("arbitrary", "arbitrary"), vmem_limit_bytes=VMEM_LIMIT),
        name="dsa",
    )(iq, small, qa, ik4, karep, varep)


C_EXT = MIX_W + LANES


def _log_sigmoid(x):
    return jnp.minimum(x, 0.0) - jnp.log(1.0 + jnp.exp(-jnp.abs(x)))


def _lower_tri():
    r = np.arange(CHUNK)
    return jnp.asarray((r[None, :] <= r[:, None]).astype(np.float32), dtype=bf16)


def _mlstm_kernel(q_ref, k_ref, v_ref, co_ref, small_ref, bias_ref, cnorm_ref, ltri_ref, s0_ref, m0_ref,
                  o_ref, s_ref, m_ref):
    c = CHUNK
    i = pl.program_id(1)

    @pl.when(i == 0)
    def _():
        s_ref[...] = s0_ref[...]
        m_ref[...] = m0_ref[...]

    lane = lax.broadcasted_iota(i32, (c, LANES), 1)
    g0 = small_ref[0] + bias_ref[...]
    g = jnp.where((lane >= S_CF) & (lane < S_CF + N_HEADS), _log_sigmoid(g0), g0)
    bc = _dot_exact_lhs(ltri_ref[...], g)
    gt, bt = g.T, bc.T
    q, k, v = q_ref[0], k_ref[0], v_ref[0]
    s_all = _nt(_stack_heads(q), k)
    s_ext = s_ref[0]
    m_all = m_ref[0]
    inter = _dot(q, s_ext.astype(bf16))
    tpos = lax.broadcasted_iota(i32, (c, c), 0)
    spos = lax.broadcasted_iota(i32, (c, c), 1)
    num = jnp.zeros((c, MIX_W), f32)
    w_cols, hd_cols, u_cols, dec_rows, m_new_all = [], [], [], [], m_all
    for h in range(N_HEADS):
        b_col, b_row = bc[:, S_CF + h:S_CF + h + 1], bt[S_CF + h:S_CF + h + 1, :]
        i_col, i_row = g[:, S_CI + h:S_CI + h + 1], gt[S_CI + h:S_CI + h + 1, :]
        m_h = m_all[0:1, h:h + 1]
        logw = jnp.where(spos <= tpos, b_col - b_row + i_row, -jnp.inf)
        log_inter = b_col + m_h
        mt = jnp.maximum(log_inter, jnp.max(logw, axis=1, keepdims=True))
        sc = s_all[h * c:(h + 1) * c] * jnp.exp(logw - mt)
        w_inter = jnp.exp(log_inter - mt)
        den = jnp.sum(sc, axis=1, keepdims=True) + w_inter * inter[:, MIX_W + h:MIX_W + h + 1]
        vh = jnp.where(_head_lane_mask(v.shape, h), v, jnp.zeros_like(v))
        num = num + _dot(sc.astype(bf16), vh)
        w_cols.append(w_inter)
        hd_cols.append(1.0 / jnp.maximum(jnp.abs(den), jnp.exp(-mt)))
        b_end = bc[c - 1:c, S_CF + h:S_CF + h + 1]
        m_new = jnp.maximum(b_end + m_h, jnp.max(b_end - b_row + i_row, axis=1, keepdims=True))
        u_cols.append(jnp.exp(b_end - b_col + i_col - m_new))
        dec_rows.append(jnp.exp(b_end + m_h - m_new))
        hl = lax.broadcasted_iota(i32, m_all.shape, 1)
        m_new_all = jnp.where(hl == h, m_new, m_new_all)
    hc = (num + _expand_heads(w_cols, c) * inter[:, :MIX_W]) * _expand_heads(hd_cols, c)
    sq = hc * hc
    rs = [lax.rsqrt(jnp.sum(jnp.where(_head_lane_mask(sq.shape, h), sq, 0.0), axis=1, keepdims=True) * (1.0 / HEAD_DIM) + EPS)
          for h in range(N_HEADS)]
    o_ref[0] = (jax.nn.sigmoid(co_ref[0]) * (hc * _expand_heads(rs, c)) * cnorm_ref[...]).astype(o_ref.dtype)

    ku = (k.astype(f32) * _expand_heads(u_cols, c)).astype(bf16)
    upd = _tn(ku, jnp.concatenate([v, jnp.ones((c, LANES), bf16)], axis=1))
    rh = lax.broadcasted_iota(i32, (MIX_W, C_EXT), 0) // HEAD_DIM
    cl = lax.broadcasted_iota(i32, (MIX_W, C_EXT), 1)
    struct = jnp.where(cl < MIX_W, cl // HEAD_DIM, cl - MIX_W) == rh
    dec = jnp.broadcast_to(dec_rows[3], (MIX_W, C_EXT))
    for h in (2, 1, 0):
        dec = jnp.where(rh == h, jnp.broadcast_to(dec_rows[h], (MIX_W, C_EXT)), dec)
    s_ref[0] = dec * s_ext + jnp.where(struct, upd, 0.0)
    m_ref[0] = m_new_all


def _mlstm_prompt(q, k, v, co, small, bias_row, cnorm, s0, m0):
    b, t, _ = q.shape
    blk = lambda bb, i: (bb, i, 0)
    st = lambda bb, i: (bb, 0, 0)
    const = lambda bb, i: (0, 0)
    return pl.pallas_call(
        _mlstm_kernel,
        grid=(b, t // CHUNK),
        in_specs=[pl.BlockSpec((1, CHUNK, MIX_W), blk)] * 4 + [pl.BlockSpec((1, CHUNK, LANES), blk),
                  pl.BlockSpec((1, LANES), const), pl.BlockSpec((1, MIX_W), const), pl.BlockSpec((CHUNK, CHUNK), const),
                  pl.BlockSpec((1, MIX_W, C_EXT), st), pl.BlockSpec((1, SUBLANES, LANES), st)],
        out_specs=[pl.BlockSpec((1, CHUNK, MIX_W), blk), pl.BlockSpec((1, MIX_W, C_EXT), st), pl.BlockSpec((1, SUBLANES, LANES), st)],
        out_shape=[jax.ShapeDtypeStruct((b, t, MIX_W), bf16), jax.ShapeDtypeStruct((b, MIX_W, C_EXT), f32),
                   jax.ShapeDtypeStruct((b, SUBLANES, LANES), f32)],
        compiler_params=pltpu.CompilerParams(dimension_semantics=("arbitrary", "arbitrary"), vmem_limit_bytes=VMEM_LIMIT),
        name="mlstm",
    )(q, k, v, co, small, bias_row, cnorm, _lower_tri(), s0, m0)


def _mlstm_state_to_ext(c0, n0, m0):
    b = c0.shape[0]
    eye = jnp.eye(N_HEADS, dtype=f32)
    blocks = jnp.einsum("bhed,hg->bhdge", c0, eye).reshape(b, MIX_W, MIX_W)
    ncols = jnp.einsum("bhd,hg->bhdg", n0, eye).reshape(b, MIX_W, N_HEADS)
    s_ext = jnp.concatenate([blocks, ncols, jnp.zeros((b, MIX_W, LANES - N_HEADS), f32)], axis=-1)
    m = jnp.zeros((b, SUBLANES, LANES), f32).at[:, 0, :N_HEADS].set(m0)
    return s_ext, m


def _mlstm_state_from_ext(s_ext, m):
    b = s_ext.shape[0]
    blocks = s_ext[:, :, :MIX_W].reshape(b, N_HEADS, HEAD_DIM, N_HEADS, HEAD_DIM)
    c1 = jnp.stack([blocks[:, h, :, h, :] for h in range(N_HEADS)], axis=1)
    ncols = s_ext[:, :, MIX_W:MIX_W + N_HEADS].reshape(b, N_HEADS, HEAD_DIM, N_HEADS)
    n1 = jnp.stack([ncols[:, h, :, h] for h in range(N_HEADS)], axis=1)
    return jnp.swapaxes(c1, 2, 3), n1, m[:, 0, :N_HEADS]


def _ssd_state_to_ext(h0):
    b = h0.shape[0]
    return h0.reshape(b, D_GROUPS, 2, HEAD_DIM, D_STATE).transpose(0, 1, 4, 2, 3).reshape(b, D_GROUPS, D_STATE, LANES)


def _ssd_state_from_ext(ht):
    b = ht.shape[0]
    return ht.reshape(b, D_GROUPS, D_STATE, 2, HEAD_DIM).transpose(0, 1, 3, 4, 2).reshape(b, N_HEADS, HEAD_DIM, D_STATE)


def _expand2(c0, c1, rows):
    lane = lax.broadcasted_iota(i32, (rows, LANES), 1)
    return jnp.where(lane < HEAD_DIM, jnp.broadcast_to(c0, (rows, LANES)), jnp.broadcast_to(c1, (rows, LANES)))


def _softplus(x):
    return jnp.maximum(x, 0.0) + jnp.log(1.0 + jnp.exp(-jnp.abs(x)))


def _ssd_kernel(x_ref, halo_ref, conv0_ref, dz_ref, small_ref, convw_ref, dtb_ref, alog_ref, dd_ref, dnorm_ref, ltri_ref, h0_ref,
                o_ref, ht_ref, xcat_ref):
    c = CHUNK
    i = pl.program_id(1)

    @pl.when(i == 0)
    def _():
        ht_ref[...] = h0_ref[...]
        xcat_ref[0:SUBLANES, :] = conv0_ref[0]

    @pl.when(i > 0)
    def _():
        xcat_ref[0:SUBLANES, :] = halo_ref[0]

    xcat_ref[SUBLANES:SUBLANES + c, :] = x_ref[0]
    w = convw_ref[...]
    y = w[D_CONV:D_CONV + 1, :] + w[0:1, :] * xcat_ref[pl.ds(SUBLANES - 3, c), :]
    for j in range(1, D_CONV):
        y = y + w[j:j + 1, :] * xcat_ref[pl.ds(SUBLANES - 3 + j, c), :]
    xbc = y * jax.nn.sigmoid(y)
    dx, dbm, dcm = xbc[:, :MIX_W], xbc[:, MIX_W:2 * MIX_W], xbc[:, 2 * MIX_W:]
    dxb = dx.astype(bf16)

    dt = _softplus(small_ref[0] + dtb_ref[...])
    a = dt * (-jnp.exp(alog_ref[...]))
    ac = _dot_exact_lhs(ltri_ref[...], a)
    dtt, act = dt.T, ac.T
    tpos = lax.broadcasted_iota(i32, (c, c), 0)
    spos = lax.broadcasted_iota(i32, (c, c), 1)
    y_all = jnp.zeros((c, MIX_W), f32)
    y_inter, new_state = [], []
    for g in range(D_GROUPS):
        bg = dbm[:, g * D_STATE:(g + 1) * D_STATE].astype(bf16)
        cg = dcm[:, g * D_STATE:(g + 1) * D_STATE].astype(bf16)
        cb = _nt(cg, bg)
        e_cols, w_cols, d_ends = [], [], []
        for h in (2 * g, 2 * g + 1):
            a_col, a_row = ac[:, S_DT + h:S_DT + h + 1], act[S_DT + h:S_DT + h + 1, :]
            dt_col, dt_row = dt[:, S_DT + h:S_DT + h + 1], dtt[S_DT + h:S_DT + h + 1, :]
            decay = jnp.exp(jnp.where(spos <= tpos, a_col - a_row, -jnp.inf))
            xh = jnp.where(_head_lane_mask(dxb.shape, h), dxb, jnp.zeros_like(dxb))
            y_all = y_all + _dot((cb * decay * dt_row).astype(bf16), xh)
            a_end = ac[c - 1:c, S_DT + h:S_DT + h + 1]
            e_cols.append(jnp.exp(a_col))
            w_cols.append(jnp.exp(a_end - a_col) * dt_col)
            d_ends.append(jnp.exp(a_end))
        ht = ht_ref[0, g]
        y_inter.append(_expand2(e_cols[0], e_cols[1], c) * _dot(cg, ht.astype(bf16)))
        xw = (dx[:, g * LANES:(g + 1) * LANES] * _expand2(w_cols[0], w_cols[1], c)).astype(bf16)
        new_state.append(_expand2(d_ends[0], d_ends[1], D_STATE) * ht + _tn(bg, xw))
    for g in range(D_GROUPS):
        ht_ref[0, g] = new_state[g]
    yd = y_all + jnp.concatenate(y_inter, axis=1) + dd_ref[...] * dx
    dz = dz_ref[0]
    o_ref[0] = _rms(yd * (dz * jax.nn.sigmoid(dz)), dnorm_ref[...]).astype(o_ref.dtype)


def _ssd_prompt(dxbc, conv0, dz, small, convw, dtb_row, alog_row, dd_row, dnorm, h0):
    b, t, _ = dxbc.shape
    blk = lambda bb, i: (bb, i, 0)
    st = lambda bb, i: (bb, 0, 0)
    const = lambda bb, i: (0, 0)
    per_chunk = CHUNK // SUBLANES
    return pl.pallas_call(
        _ssd_kernel,
        grid=(b, t // CHUNK),
        in_specs=[pl.BlockSpec((1, CHUNK, D_CONV_CH), blk),
                  pl.BlockSpec((1, SUBLANES, D_CONV_CH), lambda bb, i: (bb, jnp.maximum(i * per_chunk - 1, 0), 0)),
                  pl.BlockSpec((1, SUBLANES, D_CONV_CH), st),
                  pl.BlockSpec((1, CHUNK, MIX_W), blk), pl.BlockSpec((1, CHUNK, LANES), blk),
                  pl.BlockSpec((SUBLANES, D_CONV_CH), const), pl.BlockSpec((1, LANES), const), pl.BlockSpec((1, LANES), const),
                  pl.BlockSpec((1, MIX_W), const), pl.BlockSpec((1, MIX_W), const), pl.BlockSpec((CHUNK, CHUNK), const),
                  pl.BlockSpec((1, D_GROUPS, D_STATE, LANES), lambda bb, i: (bb, 0, 0, 0))],
        out_specs=[pl.BlockSpec((1, CHUNK, MIX_W), blk), pl.BlockSpec((1, D_GROUPS, D_STATE, LANES), lambda bb, i: (bb, 0, 0, 0))],
        out_shape=[jax.ShapeDtypeStruct((b, t, MIX_W), bf16), jax.ShapeDtypeStruct((b, D_GROUPS, D_STATE, LANES), f32)],
        scratch_shapes=[pltpu.VMEM((SUBLANES + CHUNK, D_CONV_CH), f32)],
        compiler_params=pltpu.CompilerParams(dimension_semantics=("arbitrary", "arbitrary"), vmem_limit_bytes=VMEM_LIMIT),
        name="ssd",
    )(dxbc, dxbc, conv0, dz, small, convw, dtb_row, alog_row, dd_row, dnorm, _lower_tri(), h0)


def _resident(shape):
    nd = len(shape)
    return pl.BlockSpec(shape, lambda *_: (0,) * nd, pipeline_mode=pl.Buffered(1))


def _merge_kernel(x_ref, gpre_ref, wg_ref, oa_ref, ob_ref, oc_ref, od_ref, wbr_ref, wout_ref, gpost_ref, o_ref):
    x = x_ref[...]
    xn = _rms(x, gpre_ref[...]).astype(bf16)
    merged = None
    for k, br in enumerate((oa_ref, ob_ref, oc_ref, od_ref)):
        gate = jax.nn.sigmoid(_dot(xn, wg_ref[:, k * D_MODEL:(k + 1) * D_MODEL]))
        term = gate * _dot(br[...], wbr_ref[k])
        merged = term if merged is None else merged + term
    mix = _dot(merged.astype(bf16), wout_ref[...])
    o_ref[...] = x + _rms(mix, gpost_ref[...])


def _merge(x, gpre, wg, oa, ob, oc, od, wbr, wout, gpost, tm):
    n = x.shape[0]
    row = lambda i: (i, 0)
    return pl.pallas_call(
        _merge_kernel,
        grid=(n // tm,),
        in_specs=[pl.BlockSpec((tm, D_MODEL), row), _resident((1, D_MODEL)), _resident((D_MODEL, N_HEADS * D_MODEL))]
        + [pl.BlockSpec((tm, MIX_W), row)] * 4
        + [_resident((N_HEADS, MIX_W, D_MODEL)), _resident((D_MODEL, D_MODEL)), _resident((1, D_MODEL))],
        out_specs=pl.BlockSpec((tm, D_MODEL), row),
        out_shape=jax.ShapeDtypeStruct((n, D_MODEL), f32),
        compiler_params=pltpu.CompilerParams(dimension_semantics=("arbitrary",), vmem_limit_bytes=VMEM_LIMIT),
        name="merge",
    )(x, gpre, wg, oa, ob, oc, od, wbr, wout, gpost)


FF_BLOCK = 256
FF_NBLK = D_FF // FF_BLOCK


def _gelu_tanh(x):
    return 0.5 * x * (1.0 + jnp.tanh(math.sqrt(2.0 / math.pi) * (x + 0.044715 * (x * x * x))))


def _ffn_finish(x, h_ref, down_ref, gpost_ref, o_ref):
    f = _dot(h_ref[...], down_ref[...])
    o_ref[...] = x + _rms(f, gpost_ref[...])


def _ffn_prompt_kernel(x_ref, gpre_ref, up_ref, cw_ref, down_ref, gpost_ref, o_ref, tail_ref,
                       halo_ref, ucat_ref, h_ref, *, tiles_per_seq):
    x = x_ref[...]
    tm = x.shape[0]
    xn = _rms(x, gpre_ref[...]).astype(bf16)
    first = (pl.program_id(0) % tiles_per_seq) == 0
    for j in range(FF_NBLK):
        conv = []
        for part in (j, FF_NBLK + j):
            cols = slice(part * FF_BLOCK, (part + 1) * FF_BLOCK)
            u = _dot(xn, up_ref[:, cols])
            ucat_ref[0:SUBLANES, :] = jnp.where(first, 0.0, halo_ref[part])
            ucat_ref[SUBLANES:SUBLANES + tm, :] = u
            cw = cw_ref[:, cols]
            conv.append(cw[FFN_CONV:FFN_CONV + 1] + cw[0:1] * ucat_ref[pl.ds(SUBLANES - 2, tm), :]
                        + cw[1:2] * ucat_ref[pl.ds(SUBLANES - 1, tm), :] + cw[2:3] * u)
            halo_ref[part] = u[tm - SUBLANES:, :]
            tail_ref[0, :, cols] = u[tm - SUBLANES:, :]
        h_ref[:, j * FF_BLOCK:(j + 1) * FF_BLOCK] = (conv[0] * _gelu_tanh(conv[1])).astype(bf16)
    _ffn_finish(x, h_ref, down_ref, gpost_ref, o_ref)


def _ffn_decode_kernel(x_ref, gpre_ref, up_ref, cw_ref, down_ref, gpost_ref, buf_ref, o_ref, tail_ref, h_ref):
    x = x_ref[...]
    xn = _rms(x, gpre_ref[...]).astype(bf16)
    for j in range(FF_NBLK):
        conv = []
        for part in (j, FF_NBLK + j):
            cols = slice(part * FF_BLOCK, (part + 1) * FF_BLOCK)
            u = _dot(xn, up_ref[:, cols])
            cw = cw_ref[:, cols]
            b0, b1 = buf_ref[0, :, cols], buf_ref[1, :, cols]
            conv.append(cw[FFN_CONV:FFN_CONV + 1] + cw[0:1] * b0 + cw[1:2] * b1 + cw[2:3] * u)
            tail_ref[0, :, cols] = b1
            tail_ref[1, :, cols] = u
        h_ref[:, j * FF_BLOCK:(j + 1) * FF_BLOCK] = (conv[0] * _gelu_tanh(conv[1])).astype(bf16)
    _ffn_finish(x, h_ref, down_ref, gpost_ref, o_ref)


def _ffn_prompt(x, gpre, up, cw, down, gpost, seq_len, tm):
    n = x.shape[0]
    tiles_per_seq = seq_len // tm
    row = lambda i: (i, 0)
    return pl.pallas_call(
        functools.partial(_ffn_prompt_kernel, tiles_per_seq=tiles_per_seq),
        grid=(n // tm,),
        in_specs=[pl.BlockSpec((tm, D_MODEL), row), _resident((1, D_MODEL)), _resident((D_MODEL, 2 * D_FF)),
                  _resident((SUBLANES, 2 * D_FF)), _resident((D_FF, D_MODEL)), _resident((1, D_MODEL))],
        out_specs=[pl.BlockSpec((tm, D_MODEL), row), pl.BlockSpec((1, SUBLANES, 2 * D_FF), lambda i: (i // tiles_per_seq, 0, 0))],
        out_shape=[jax.ShapeDtypeStruct((n, D_MODEL), f32), jax.ShapeDtypeStruct((n // seq_len, SUBLANES, 2 * D_FF), f32)],
        scratch_shapes=[pltpu.VMEM((2 * FF_NBLK, SUBLANES, FF_BLOCK), f32), pltpu.VMEM((SUBLANES + tm, FF_BLOCK), f32),
                        pltpu.VMEM((tm, D_FF), bf16)],
        compiler_params=pltpu.CompilerParams(dimension_semantics=("arbitrary",), vmem_limit_bytes=VMEM_LIMIT),
        name="ffn_prompt",
    )(x, gpre, up, cw, down, gpost)


def _ffn_decode(x, gpre, up, cw, down, gpost, buf):
    n = x.shape[0]
    return pl.pallas_call(
        _ffn_decode_kernel,
        grid=(1,),
        in_specs=[_resident((n, D_MODEL)), _resident((1, D_MODEL)), _resident((D_MODEL, 2 * D_FF)),
                  _resident((SUBLANES, 2 * D_FF)), _resident((D_FF, D_MODEL)), _resident((1, D_MODEL)),
                  _resident((FFN_CONV - 1, n, 2 * D_FF))],
        out_specs=[pl.BlockSpec((n, D_MODEL), lambda i: (0, 0)), pl.BlockSpec((FFN_CONV - 1, n, 2 * D_FF), lambda i: (0, 0, 0))],
        out_shape=[jax.ShapeDtypeStruct((n, D_MODEL), f32), jax.ShapeDtypeStruct((FFN_CONV - 1, n, 2 * D_FF), f32)],
        scratch_shapes=[pltpu.VMEM((n, D_FF), bf16)],
        compiler_params=pltpu.CompilerParams(dimension_semantics=("arbitrary",), vmem_limit_bytes=VMEM_LIMIT),
        name="ffn_decode",
    )(x, gpre, up, cw, down, gpost, buf)


PAGES_PER_STEP = 8


def _page_specs(layer, pg, block):
    nd = len(block)

    def spec(k):
        return pl.BlockSpec((1, 1) + block, lambda b, g, pt: (layer, pt[b, g * pg + k]) + (0,) * nd)

    return [spec(k) for k in range(pg)]


def _dec_scores_kernel(pt_ref, iq_ref, iw_ref, *refs):
    pages, o_ref = refs[:-1], refs[-1]
    iq = iq_ref[0]
    iw = iw_ref[0]
    ps = pages[0].shape[3]
    for k, pr in enumerate(pages):
        s = _dot(iq, pr[0, 0].astype(bf16))
        o_ref[0, :, k * ps:(k + 1) * ps] = jnp.sum(iw * jnp.maximum(s, 0.0), axis=0, keepdims=True)


def _dec_scores(layer, page_table, iq8, iw8, pool):
    s, npages = page_table.shape
    dim, ps = pool.shape[2], pool.shape[3]
    pg = min(PAGES_PER_STEP, npages)
    grid_spec = pltpu.PrefetchScalarGridSpec(
        num_scalar_prefetch=1, grid=(s, npages // pg),
        in_specs=[pl.BlockSpec((1, SUBLANES, dim), lambda b, g, pt: (b, 0, 0)),
                  pl.BlockSpec((1, SUBLANES, 1), lambda b, g, pt: (b, 0, 0))] + _page_specs(layer, pg, (dim, ps)),
        out_specs=pl.BlockSpec((1, 1, pg * ps), lambda b, g, pt: (b, 0, g)))
    return pl.pallas_call(
        _dec_scores_kernel, grid_spec=grid_spec,
        out_shape=jax.ShapeDtypeStruct((s, 1, npages * ps), f32),
        compiler_params=pltpu.CompilerParams(dimension_semantics=("arbitrary", "arbitrary"), vmem_limit_bytes=VMEM_LIMIT),
        name="dec_scores",
    )(page_table, iq8, iw8, *([pool] * pg))


def _dec_select_kernel(sc_in_ref, iq_ref, iw_ref, ik_ref, thr_ref, j_ref, snew_ref, sc_ref, *, topk):
    nc, s, _ = sc_in_ref.shape

    def copy(c, carry):
        sc_ref[c] = sc_in_ref[c]
        return carry

    lax.fori_loop(0, nc, copy, 0)
    prod = iq_ref[...].astype(f32) * ik_ref[...].astype(f32)
    dots = jnp.sum(prod, axis=2, keepdims=True)
    snew = jnp.sum(iw_ref[...] * jnp.maximum(dots, 0.0), axis=1)
    snew = jnp.broadcast_to(snew, (s, LANES))
    lane = lax.broadcasted_iota(i32, (s, KV_BLOCK), 1)
    sc_ref[nc] = jnp.where(lane == 0, _tile2(snew), -jnp.inf)
    idx_bits = max(1, int(math.ceil(math.log2((nc + 1) * KV_BLOCK)))) + 1
    thr = _select_topk(sc_ref, nc + 1, jnp.int32(0), topk, j_ref, idx_bits)
    thr_ref[...] = thr
    snew_ref[...] = snew


def _dec_select(scores, iq8, iw8, ik_new, topk):
    s, _, p = scores.shape
    nc = p // KV_BLOCK
    sc = scores.reshape(s, nc, KV_BLOCK).transpose(1, 0, 2)
    return pl.pallas_call(
        functools.partial(_dec_select_kernel, topk=topk),
        grid=(1,),
        in_specs=[_resident((nc, s, KV_BLOCK)), _resident(iq8.shape), _resident(iw8.shape), _resident((s, 1, iq8.shape[2]))],
        out_specs=[pl.BlockSpec((s, LANES), lambda i: (0, 0))] * 3,
        out_shape=[jax.ShapeDtypeStruct((s, LANES), f32), jax.ShapeDtypeStruct((s, LANES), i32), jax.ShapeDtypeStruct((s, LANES), f32)],
        scratch_shapes=[pltpu.VMEM((nc + 1, s, KV_BLOCK), f32)],
        compiler_params=pltpu.CompilerParams(dimension_semantics=("arbitrary",), vmem_limit_bytes=VMEM_LIMIT),
        name="dec_select",
    )(sc, iq8, iw8, ik_new[:, None, :])


def _dec_attn_kernel(pt_ref, qm_ref, sc_ref, thr_ref, j_ref, snew_ref, knew_ref, vnew_ref, *refs, npast):
    pg = (len(refs) - 4) // 2
    kpages, vpages = refs[:pg], refs[pg:2 * pg]
    o_ref, m_ref, l_ref, acc_ref = refs[2 * pg:]
    g = pl.program_id(1)
    ps = kpages[0].shape[3]

    @pl.when(g == 0)
    def _():
        m_ref[...] = jnp.full_like(m_ref, NEG)
        l_ref[...] = jnp.zeros_like(l_ref)
        acc_ref[...] = jnp.zeros_like(acc_ref)

    qm = qm_ref[0]
    thr, jcut = thr_ref[0], j_ref[0]

    def update(z, sel):
        zz = jnp.where(sel, z, NEG)
        m_old = m_ref[...]
        m_new = jnp.maximum(m_old, jnp.max(zz, axis=1, keepdims=True))
        p = jnp.where(sel, jnp.exp(zz - m_new), 0.0)
        alpha = jnp.exp(m_old - m_new)
        l_ref[...] = alpha * l_ref[...] + jnp.sum(p, axis=1, keepdims=True)
        m_ref[...] = m_new
        return alpha, p

    s = sc_ref[0]
    kidx = g * (pg * ps) + lax.broadcasted_iota(i32, s.shape, 1)
    thr_t, jcut_t = jnp.concatenate([thr] * pg, axis=1), jnp.concatenate([jcut] * pg, axis=1)
    sel = (s > thr_t) | ((s == thr_t) & (kidx < jcut_t))
    z = jnp.concatenate([_dot(qm, kp[0, 0].astype(bf16)) for kp in kpages], axis=1)
    alpha, p = update(z, sel)
    pb = p.astype(bf16)
    acc = alpha * acc_ref[...]
    for k in range(pg):
        acc = acc + _nt(pb[:, k * ps:(k + 1) * ps], vpages[k][0, 0].astype(bf16))
    acc_ref[...] = acc

    @pl.when(g == pl.num_programs(1) - 1)
    def _():
        snew = snew_ref[0]
        sel = ((snew > thr) | ((snew == thr) & (jnp.int32(npast) < jcut)))[:, 0:1]
        z = jnp.sum(qm.astype(f32) * knew_ref[0], axis=1, keepdims=True)
        alpha, p = update(z, sel)
        acc = alpha * acc_ref[...] + p * vnew_ref[0]
        o_ref[0] = acc / l_ref[...]


def _dec_attn(layer, page_table, qm, scores, thr, jcut, snew, k_new, v_new, kpool, vpool):
    s, npages = page_table.shape
    ps = kpool.shape[3]
    pg = min(PAGES_PER_STEP, npages)
    row = lambda b, g, pt: (b, 0, 0)
    rows = lambda a: a[:, None, :]
    grid_spec = pltpu.PrefetchScalarGridSpec(
        num_scalar_prefetch=1, grid=(s, npages // pg),
        in_specs=[pl.BlockSpec((1, SUBLANES, LANES), row), pl.BlockSpec((1, 1, pg * ps), lambda b, g, pt: (b, 0, g))]
        + [pl.BlockSpec((1, 1, LANES), row)] * 5
        + _page_specs(layer, pg, kpool.shape[2:]) + _page_specs(layer, pg, vpool.shape[2:]),
        out_specs=pl.BlockSpec((1, SUBLANES, LANES), row),
        scratch_shapes=[pltpu.VMEM((SUBLANES, 1), f32), pltpu.VMEM((SUBLANES, 1), f32), pltpu.VMEM((SUBLANES, LANES), f32)])
    return pl.pallas_call(
        functools.partial(_dec_attn_kernel, npast=npages * ps), grid_spec=grid_spec,
        out_shape=jax.ShapeDtypeStruct((s, SUBLANES, LANES), f32),
        compiler_params=pltpu.CompilerParams(dimension_semantics=("arbitrary", "arbitrary"), vmem_limit_bytes=VMEM_LIMIT),
        name="dec_attn",
    )(page_table, qm, scores, rows(thr), rows(jcut), rows(snew), rows(k_new), rows(v_new), *([kpool] * pg), *([vpool] * pg))


def _dec_sb_kernel(pt_ref, q_ref, uo_ref, *refs):
    pg = (len(refs) - 3) // 2
    kpages, vpages = refs[:pg], refs[pg:2 * pg]
    o_ref, r_ref, acc_ref = refs[2 * pg:]
    g = pl.program_id(1)

    @pl.when(g == 0)
    def _():
        r_ref[...] = jnp.zeros_like(r_ref)
        acc_ref[...] = jnp.zeros_like(acc_ref)

    q = jnp.broadcast_to(q_ref[0].astype(f32), (SUBLANES, MIX_W))
    rowh = lax.broadcasted_iota(i32, (SUBLANES, MIX_W), 0)
    laneh = lax.broadcasted_iota(i32, (SUBLANES, MIX_W), 1) // HEAD_DIM
    diag = rowh == laneh
    qm = jnp.where(diag, q, 0.0).astype(bf16)
    uo = uo_ref[...]
    for k in reversed(range(pg)):
        z = _dot(qm, kpages[k][0, 0].astype(bf16))
        lk = -_softplus(z)
        hi = lk.astype(bf16)
        lo = (lk - hi.astype(f32)).astype(bf16)
        res = _dot(jnp.concatenate([hi, lo], axis=1), uo)
        r = r_ref[...]
        a = jnp.exp(z + lk + res[:, :LANES] + r)
        r_ref[...] = r + res[:, LANES:]
        acc_ref[...] += _nt(a.astype(bf16), vpages[k][0, 0].astype(bf16))

    @pl.when(g == pl.num_programs(1) - 1)
    def _():
        o_ref[0] = jnp.sum(jnp.where(diag, acc_ref[...], 0.0), axis=0, keepdims=True).astype(o_ref.dtype)


def _dec_stickbreak(layer, page_table, q, kpool, vpool):
    s, npages = page_table.shape
    ps = kpool.shape[3]
    assert ps == LANES
    pg = min(PAGES_PER_STEP, npages)
    ng = npages // pg

    def rev_specs(pool):
        nd = len(pool.shape) - 2
        return [pl.BlockSpec((1, 1) + pool.shape[2:], (lambda b, g, pt, k=k: (layer, pt[b, (ng - 1 - g) * pg + k]) + (0,) * nd))
                for k in range(pg)]

    grid_spec = pltpu.PrefetchScalarGridSpec(
        num_scalar_prefetch=1, grid=(s, ng),
        in_specs=[pl.BlockSpec((1, 1, MIX_W), lambda b, g, pt: (b, 0, 0)),
                  pl.BlockSpec((2 * LANES, 2 * LANES), lambda b, g, pt: (0, 0))] + rev_specs(kpool) + rev_specs(vpool),
        out_specs=pl.BlockSpec((1, 1, MIX_W), lambda b, g, pt: (b, 0, 0)),
        scratch_shapes=[pltpu.VMEM((SUBLANES, LANES), f32), pltpu.VMEM((SUBLANES, MIX_W), f32)])
    out = pl.pallas_call(
        _dec_sb_kernel, grid_spec=grid_spec,
        out_shape=jax.ShapeDtypeStruct((s, 1, MIX_W), bf16),
        compiler_params=pltpu.CompilerParams(dimension_semantics=("arbitrary", "arbitrary"), vmem_limit_bytes=VMEM_LIMIT),
        name="dec_stickbreak",
    )(page_table, q[:, None, :], _suffix_ones(), *([kpool] * pg), *([vpool] * pg))
    return out[:, 0, :]


def _dec_rec_kernel(q_ref, k_ref, vcol_ref, cocol_ref, cnormcol_ref, small_ref, cbias_ref, cst_ref, nst_ref, mst_ref,
                    xrow_ref, convst_ref, convw_ref, xcol_ref, convstcol_ref, convwcol_ref, dzcol_ref, dnormcol_ref,
                    ddcol_ref, dtb_ref, alog_ref, hst_ref,
                    oc_ref, cnew_ref, nnew_ref, mnew_ref, od_ref, convnew_ref, hnew_ref):
    small = small_ref[0]
    q_row, k_row, v_col = q_ref[0], k_ref[0], vcol_ref[0]
    gates = small + cbias_ref[...]
    lane = lax.broadcasted_iota(i32, (1, LANES), 1)
    m_row = jnp.zeros((1, LANES), f32)
    for h in range(N_HEADS):
        hs = slice(h * HEAD_DIM, (h + 1) * HEAD_DIM)
        q_h, k_h, v_h = q_row[:, hs], k_row[:, hs], v_col[hs, :]
        cm = cst_ref[0, 0, h]
        n_h = nst_ref[0, 0, h:h + 1, :]
        m_h = mst_ref[0, 0, :, h:h + 1]
        ig = gates[:, S_CI + h:S_CI + h + 1]
        lf = _log_sigmoid(gates[:, S_CF + h:S_CF + h + 1])
        log_inter = lf + m_h
        mt = jnp.maximum(log_inter, ig)
        sc = jnp.sum(q_h * k_h, axis=1, keepdims=True) * jnp.exp(ig - mt)
        w_inter = jnp.exp(log_inter - mt)
        num = sc * v_h + w_inter * jnp.sum(cm * q_h, axis=1, keepdims=True)
        den = sc + w_inter * jnp.sum(n_h * q_h, axis=1, keepdims=True)
        hh = num / jnp.maximum(jnp.abs(den), jnp.exp(-mt))
        hn = hh * lax.rsqrt(jnp.mean(hh * hh, axis=0, keepdims=True) + EPS)
        oc_ref[0, hs, :] = jax.nn.sigmoid(cocol_ref[0, hs, :]) * hn * cnormcol_ref[hs, :]
        u = jnp.exp(ig - mt)
        dec = jnp.exp(lf + m_h - mt)
        cnew_ref[0, h] = dec * cm + (u * v_h) * k_h
        nnew_ref[0, h:h + 1, :] = dec * n_h + u * k_h
        m_row = jnp.where(lane == h, mt, m_row)
    mnew_ref[0] = m_row[:, :N_HEADS]
    cs = convst_ref[0, 0]
    xn = xrow_ref[0]
    w = convw_ref[...]
    y = w[D_CONV:D_CONV + 1] + w[D_CONV - 1:D_CONV] * xn
    for j in range(D_CONV - 1):
        y = y + w[j:j + 1] * cs[j:j + 1]
    xbc = y * jax.nn.sigmoid(y)
    convnew_ref[0, 0:D_CONV - 2, :] = cs[1:D_CONV - 1]
    convnew_ref[0, D_CONV - 2:D_CONV - 1, :] = xn
    ycol = convwcol_ref[D_CONV] + convwcol_ref[D_CONV - 1] * xcol_ref[0]
    for j in range(D_CONV - 1):
        ycol = ycol + convwcol_ref[j] * convstcol_ref[0, j]
    xcol = ycol * jax.nn.sigmoid(ycol)
    dt_row = _softplus(small + dtb_ref[...])
    a_row = dt_row * (-jnp.exp(alog_ref[...]))
    yds = []
    for h in range(N_HEADS):
        g = h // (N_HEADS // D_GROUPS)
        hs = slice(h * HEAD_DIM, (h + 1) * HEAD_DIM)
        b_g = xbc[:, MIX_W + g * D_STATE:MIX_W + (g + 1) * D_STATE]
        c_g = xbc[:, MIX_W + D_GROUPS * D_STATE + g * D_STATE:MIX_W + D_GROUPS * D_STATE + (g + 1) * D_STATE]
        dt_h = dt_row[:, S_DT + h:S_DT + h + 1]
        ea = jnp.exp(a_row[:, S_DT + h:S_DT + h + 1])
        x_h = xcol[hs, :]
        hm = hst_ref[0, 0, h]
        cb = jnp.sum(c_g * b_g, axis=1, keepdims=True)
        yh = cb * dt_h * x_h + ea * jnp.sum(hm * c_g, axis=1, keepdims=True)
        yds.append(yh + ddcol_ref[hs, :] * x_h)
        hnew_ref[0, h] = ea * hm + (dt_h * x_h) * b_g
    dz = dzcol_ref[0]
    t = jnp.concatenate(yds, axis=0) * (dz * jax.nn.sigmoid(dz))
    od_ref[0] = t * lax.rsqrt(jnp.mean(t * t, axis=0, keepdims=True) + EPS) * dnormcol_ref[...]


def _dec_recurrent(layer, q, k, v, co, cnorm, small, cbias_row, c_st, n_st, m_st,
                   dxbc, conv_st, convw, dz, dnorm, dd, dtb_row, alog_row, h_st):
    s = q.shape[0]
    col = lambda a: a.astype(f32)[..., None]
    row3 = lambda a: a.astype(f32)[:, None, :]
    per = lambda *blk: pl.BlockSpec((1,) + blk, lambda b: (b,) + (0,) * len(blk))
    st = lambda *blk: pl.BlockSpec((1, 1) + blk, lambda b: (layer, b) + (0,) * len(blk))
    convw_col = col(convw[:D_CONV + 1, :MIX_W])
    ins = [
        (row3(q), per(1, MIX_W)), (row3(k), per(1, MIX_W)), (col(v), per(MIX_W, 1)), (col(co), per(MIX_W, 1)),
        (col(cnorm[0]), _resident((MIX_W, 1))), (row3(small), per(1, LANES)), (cbias_row, _resident((1, LANES))),
        (c_st, st(N_HEADS, HEAD_DIM, HEAD_DIM)), (n_st, st(N_HEADS, HEAD_DIM)), (m_st[:, :, None, :], st(1, N_HEADS)),
        (row3(dxbc), per(1, D_CONV_CH)), (conv_st, st(D_CONV - 1, D_CONV_CH)), (convw, _resident((SUBLANES, D_CONV_CH))),
        (col(dxbc[:, :MIX_W]), per(MIX_W, 1)), (col(conv_st[layer][:, :, :MIX_W]), per(D_CONV - 1, MIX_W, 1)),
        (convw_col, _resident((D_CONV + 1, MIX_W, 1))), (col(dz), per(MIX_W, 1)), (col(dnorm[0]), _resident((MIX_W, 1))),
        (col(dd[0]), _resident((MIX_W, 1))), (dtb_row, _resident((1, LANES))), (alog_row, _resident((1, LANES))),
        (h_st, st(N_HEADS, HEAD_DIM, D_STATE)),
    ]
    outs = [
        (jax.ShapeDtypeStruct((s, MIX_W, 1), f32), per(MIX_W, 1)),
        (jax.ShapeDtypeStruct((s, N_HEADS, HEAD_DIM, HEAD_DIM), f32), per(N_HEADS, HEAD_DIM, HEAD_DIM)),
        (jax.ShapeDtypeStruct((s, N_HEADS, HEAD_DIM), f32), per(N_HEADS, HEAD_DIM)),
        (jax.ShapeDtypeStruct((s, 1, N_HEADS), f32), per(1, N_HEADS)),
        (jax.ShapeDtypeStruct((s, MIX_W, 1), f32), per(MIX_W, 1)),
        (jax.ShapeDtypeStruct((s, D_CONV - 1, D_CONV_CH), f32), per(D_CONV - 1, D_CONV_CH)),
        (jax.ShapeDtypeStruct((s, N_HEADS, HEAD_DIM, D_STATE), f32), per(N_HEADS, HEAD_DIM, D_STATE)),
    ]
    oc, c1, n1, m1, od, conv1, h1 = pl.pallas_call(
        _dec_rec_kernel,
        grid=(s,),
        in_specs=[sp for _, sp in ins],
        out_specs=[sp for _, sp in outs],
        out_shape=[sh for sh, _ in outs],
        compiler_params=pltpu.CompilerParams(dimension_semantics=("arbitrary",), vmem_limit_bytes=VMEM_LIMIT),
        name="dec_recurrent",
    )(*[a for a, _ in ins])
    return oc[:, :, 0], c1, n1, m1[:, 0, :], od[:, :, 0], conv1, h1


_IN_SPLITS = (256, 128, 128, 256, 64, 4, 256, 256, 256, 256, 256, 256, 4, 4, 256, 256, 768, 4, 4096)
_IN_OFFS = np.concatenate([[0], np.cumsum(_IN_SPLITS)]).tolist()
(_A_Q, _A_K, _A_V, _A_IQ, _A_IK, _A_IW, _B_Q, _B_K, _B_V, _C_Q, _C_K, _C_V, _C_I, _C_F, _C_O,
 _D_Z, _D_XBC, _D_DT, _GATE) = range(len(_IN_SPLITS))


def _lane_row(width, items):
    r = jnp.zeros((1, width), f32)
    for off, v in items:
        r = r.at[0, off:off + v.shape[0]].set(v.astype(f32))
    return r


def _prep_layer(l, p):
    w_in = p["w_in"][l]
    seg = lambda i: w_in[:, _IN_OFFS[i]:_IN_OFFS[i + 1]]
    rep2 = lambda w: jnp.concatenate([w[:, :HEAD_DIM], w[:, :HEAD_DIM], w[:, HEAD_DIM:], w[:, HEAD_DIM:]], axis=1)
    small = jnp.concatenate([seg(_A_IW) * (N_HEADS ** -0.5 * HEAD_DIM ** -0.5), seg(_C_I), seg(_C_F), seg(_D_DT),
                             jnp.zeros((D_MODEL, LANES - 4 * N_HEADS), f32)], axis=1)
    scale = HEAD_DIM ** -0.5
    w_proj = jnp.concatenate([
        seg(_A_Q) * scale, rep2(seg(_A_K)), rep2(seg(_A_V)), seg(_A_IQ), jnp.tile(seg(_A_IK), (1, N_HEADS)),
        seg(_B_Q) * scale, seg(_B_K), seg(_B_V),
        seg(_C_Q), seg(_C_K) * scale, seg(_C_V), seg(_C_O),
        seg(_D_Z), seg(_D_XBC), small], axis=1).astype(bf16)
    row = lambda a: a[None, :].astype(f32)
    pad_rows = lambda a: jnp.concatenate([a, jnp.zeros((SUBLANES - a.shape[0], a.shape[1]), f32)], axis=0)
    return dict(
        w_proj=w_proj, w_gate=seg(_GATE).astype(bf16),
        g_mix_pre=row(p["norm_mix_pre"][l]), g_mix_post=row(p["norm_mix_post"][l]),
        g_ffn_pre=row(p["norm_ffn_pre"][l]), g_ffn_post=row(p["norm_ffn_post"][l]),
        cbias_row=_lane_row(LANES, [(S_CI, p["c_b_i"][l]), (S_CF, p["c_b_f"][l])]),
        cnorm=row(p["c_norm"][l]),
        convw=pad_rows(jnp.concatenate([p["d_conv_w"][l], p["d_conv_b"][l][None]], axis=0)),
        dtb_row=_lane_row(LANES, [(S_DT, p["d_dt_bias"][l])]), alog_row=_lane_row(LANES, [(S_DT, p["d_A_log"][l])]),
        dd_row=row(jnp.repeat(p["d_D"][l], HEAD_DIM)), dnorm=row(p["d_norm"][l]),
        wbr=jnp.stack([p["w_br_a"][l], p["w_br_b"][l], p["w_br_c"][l], p["w_br_d"][l]]).astype(bf16),
        wout=p["w_out"][l].astype(bf16),
        up=p["ffn_up"][l].astype(bf16), down=p["ffn_down"][l].astype(bf16),
        ffn_cw=pad_rows(jnp.concatenate([p["ffn_conv_w"][l], p["ffn_conv_b"][l][None]], axis=0)),
    )


def _rope_table(pos):
    half = HEAD_DIM // 2
    inv = jnp.power(ROPE_THETA, -jnp.arange(half, dtype=f32) / half)
    ang = pos.astype(f32)[:, None] * inv[None, :]
    cos, sin = jnp.cos(ang), jnp.sin(ang)
    return jnp.concatenate([jnp.tile(cos, (1, 4)), jnp.tile(jnp.concatenate([-sin, sin], axis=1), (1, 2))], axis=1)


def _row_tile(n):
    for tm in (512, 256, 128, 64, 32, 16, 8):
        if n % tm == 0:
            return tm
    raise ValueError(f"row count {n} is not a multiple of 8")


def _prompt_layer(x, cs, w, bsz, t):
    n = x.shape[0]
    tm = min(_row_tile(t), _row_tile(n))
    (qa, ka, karep, va, varep, iq, ik, ik4, qb, kb, kbh, vb, vbh, qc, kc, vc, co, dz, dxbc, small) = _proj(
        x, w["g_mix_pre"], w["w_proj"], cs, tm)
    r3 = lambda a: a.reshape(bsz, t, a.shape[-1])
    small3 = r3(small)
    o_a = _dsa_prompt(r3(iq), small3, r3(qa), r3(ik4), r3(karep), r3(varep))
    o_b = _stickbreak_prompt(r3(qb), r3(kbh), r3(vbh))
    o_c, s_ext, m_ext = _mlstm_prompt(r3(qc), r3(kc), r3(vc), r3(co), small3, w["cbias_row"], w["cnorm"],
                                      jnp.zeros((bsz, MIX_W, C_EXT), f32), jnp.zeros((bsz, SUBLANES, LANES), f32))
    dxbc3 = r3(dxbc)
    o_d, ht = _ssd_prompt(dxbc3, jnp.zeros((bsz, SUBLANES, D_CONV_CH), f32), r3(dz), small3, w["convw"], w["dtb_row"],
                          w["alog_row"], w["dd_row"], w["dnorm"], jnp.zeros((bsz, D_GROUPS, D_STATE, LANES), f32))
    r2 = lambda a: a.reshape(n, MIX_W)
    x1 = _merge(x, w["g_mix_pre"], w["w_gate"], r2(o_a), r2(o_b), r2(o_c), r2(o_d), w["wbr"], w["wout"], w["g_mix_post"], tm)
    x2, tail = _ffn_prompt(x1, w["g_ffn_pre"], w["up"], w["ffn_cw"], w["down"], w["g_ffn_post"], t, tm)
    c1, n1, m1 = _mlstm_state_from_ext(s_ext, m_ext)
    state = (ka.reshape(bsz, t, 2, HEAD_DIM), va.reshape(bsz, t, 2, HEAD_DIM), ik.reshape(bsz, t, HEAD_DIM),
             kb.reshape(bsz, t, N_HEADS, HEAD_DIM), vb.reshape(bsz, t, N_HEADS, HEAD_DIM), c1, n1, m1,
             dxbc3[:, t - (D_CONV - 1):, :], _ssd_state_from_ext(ht), tail[:, SUBLANES - (FFN_CONV - 1):, :])
    return x2, state


def _sample_layer(l, x, cs, w, st):
    s = x.shape[0]
    pt = st["page_table"]
    (qa, ka, karep, va, varep, iq, ik, ik4, qb, kb, kbh, vb, vbh, qc, kc, vc, co, dz, dxbc, small) = _proj(
        x, w["g_mix_pre"], w["w_proj"], cs, s)
    pad8 = lambda a: jnp.concatenate([a, jnp.zeros((s, SUBLANES - N_HEADS) + a.shape[2:], a.dtype)], axis=1)
    iq8 = pad8(iq.reshape(s, N_HEADS, HEAD_DIM))
    iw8 = pad8(small[:, S_IW:S_IW + N_HEADS, None])
    a_kidx, a_k, a_v, b_k, b_v = st["a_kidx"], st["a_k"], st["a_v"], st["b_k"], st["b_v"]
    depth, n_pool, page = a_k.shape[:3]
    npast = pt.shape[1] * page
    tview = lambda pool: jnp.moveaxis(pool, 2, -1).reshape(depth, n_pool, -1, page)
    scores = _dec_scores(l, pt, iq8, iw8, tview(a_kidx))
    thr, jcut, snew = _dec_select(scores, iq8, iw8, ik, min(TOPK_MAX, (npast + 1) // 4))
    q4 = qa.reshape(s, N_HEADS, HEAD_DIM)
    zero = jnp.zeros_like(q4[:, 0])
    qm = pad8(jnp.stack([jnp.concatenate([q4[:, 0], zero], -1), jnp.concatenate([q4[:, 1], zero], -1),
                         jnp.concatenate([zero, q4[:, 2]], -1), jnp.concatenate([zero, q4[:, 3]], -1)], axis=1))
    o8 = _dec_attn(l, pt, qm, scores, thr, jcut, snew, ka, va, tview(a_k), tview(a_v))
    o_a = jnp.concatenate([o8[:, 0, :HEAD_DIM], o8[:, 1, :HEAD_DIM], o8[:, 2, HEAD_DIM:], o8[:, 3, HEAD_DIM:]], axis=-1).astype(bf16)
    o_b = _dec_stickbreak(l, pt, qb, tview(b_k), tview(b_v))
    o_c, c1, n1, m1, o_d, conv1, h1 = _dec_recurrent(
        l, qc, kc, vc, co, w["cnorm"], small, w["cbias_row"], st["c_C"], st["c_n"], st["c_m"],
        dxbc, st["d_conv"], w["convw"], dz, w["dnorm"], w["dd_row"], w["dtb_row"], w["alog_row"], st["d_ssm"])
    x1 = _merge(x, w["g_mix_pre"], w["w_gate"], o_a, o_b, o_c.astype(bf16), o_d.astype(bf16), w["wbr"], w["wout"],
                w["g_mix_post"], s)
    x2, buf = _ffn_decode(x1, w["g_ffn_pre"], w["up"], w["ffn_cw"], w["down"], w["g_ffn_post"],
                          jnp.swapaxes(st["ffn_conv"][l], 0, 1))
    state = (ka.reshape(s, 1, 2, HEAD_DIM), va.reshape(s, 1, 2, HEAD_DIM), ik.reshape(s, 1, HEAD_DIM),
             kb.reshape(s, 1, N_HEADS, HEAD_DIM), vb.reshape(s, 1, N_HEADS, HEAD_DIM), c1, n1, m1, conv1, h1,
             jnp.swapaxes(buf, 0, 1))
    return x2, state


def kernel(x_prompt, x_sample, cache_a_k, cache_a_v, cache_a_kidx, cache_b_k, cache_b_v, state_c_C, state_c_n, state_c_m,
           state_d_conv, state_d_ssm, state_ffn_conv, page_table, norm_mix_pre, norm_mix_post, w_in, c_b_i, c_b_f, c_norm,
           d_conv_w, d_conv_b, d_dt_bias, d_A_log, d_D, d_norm, w_br_a, w_br_b, w_br_c, w_br_d, w_out, norm_ffn_pre,
           norm_ffn_post, ffn_up, ffn_conv_w, ffn_conv_b, ffn_down):
    params = dict(norm_mix_pre=norm_mix_pre, norm_mix_post=norm_mix_post, w_in=w_in, c_b_i=c_b_i, c_b_f=c_b_f, c_norm=c_norm,
                  d_conv_w=d_conv_w, d_conv_b=d_conv_b, d_dt_bias=d_dt_bias, d_A_log=d_A_log, d_D=d_D, d_norm=d_norm,
                  w_br_a=w_br_a, w_br_b=w_br_b, w_br_c=w_br_c, w_br_d=w_br_d, w_out=w_out, norm_ffn_pre=norm_ffn_pre,
                  norm_ffn_post=norm_ffn_post, ffn_up=ffn_up, ffn_conv_w=ffn_conv_w, ffn_conv_b=ffn_conv_b, ffn_down=ffn_down)
    st = dict(page_table=page_table, a_k=cache_a_k, a_v=cache_a_v, a_kidx=cache_a_kidx, b_k=cache_b_k, b_v=cache_b_v,
              c_C=state_c_C, c_n=state_c_n, c_m=state_c_m, d_conv=state_d_conv, d_ssm=state_d_ssm, ffn_conv=state_ffn_conv)
    bsz, t, _ = x_prompt.shape
    s, ts, _ = x_sample.shape
    assert ts == 1, "the sample group is decoded one token at a time"
    depth = w_in.shape[0]
    past = page_table.shape[1] * cache_a_k.shape[2]
    cs_p = _rope_table(jnp.arange(t, dtype=i32))
    cs_s = _rope_table(jnp.full((s,), past, dtype=i32))
    xp = x_prompt.reshape(bsz * t, D_MODEL)
    xs = x_sample.reshape(s, D_MODEL)
    p_list, s_list = [], []
    for l in range(depth):
        w = _prep_layer(l, params)
        xp, sp = _prompt_layer(xp, cs_p, w, bsz, t)
        xs, ss = _sample_layer(l, xs, cs_s, w, st)
        p_list.append(sp)
        s_list.append(ss)
    p_st = [jnp.stack([sl[i] for sl in p_list]) for i in range(11)]
    s_st = [jnp.stack([sl[i] for sl in s_list]) for i in range(11)]
    return (xp.reshape(bsz, t, D_MODEL), xs.reshape(s, 1, D_MODEL), *p_st, *s_st)
```
